```python
import jax, jax.numpy as jnp
from jax import lax
import numpy as np

D_MODEL = 2048
BATCH = 4
SEQ = 4096
DEPTH = 1

POOL_WIDTH = D_MODEL // 2
POOL_WINDOWS = (2, 4, 8, 16)
N_POOL_GROUPS = len(POOL_WINDOWS)
POOL_GROUP = POOL_WIDTH // N_POOL_GROUPS
HEAD_DIM = 64
N_Q_HEADS = (D_MODEL - POOL_WIDTH) // HEAD_DIM
N_KV_HEADS = max(N_Q_HEADS // 4, 1)
Q_PER_KV = N_Q_HEADS // N_KV_HEADS
ATTN_WIDTH = N_Q_HEADS * HEAD_DIM
KV_WIDTH = N_KV_HEADS * HEAD_DIM
MIX_WIDTH = POOL_WIDTH + ATTN_WIDTH
IN_WIDTH = POOL_WIDTH + ATTN_WIDTH + 2 * KV_WIDTH
WINDOW = 128
BLOCK = 128
D_FF = (D_MODEL * 11) // 4
CONV_WIDTH = 3
EPS = 1e-6
NEG_INF = -1e30

kernel_name = "hybrid_pool_swa_alibi_convffn_encoder"


def rms_norm(x, g):
    xf = x.astype(jnp.float32)
    y = xf * lax.rsqrt(jnp.mean(xf * xf, axis=-1, keepdims=True) + EPS)
    return (y * g.astype(jnp.float32)).astype(x.dtype)


def alibi_slopes():
    i = np.arange(1, N_Q_HEADS + 1, dtype=np.float32)
    return jnp.asarray(np.power(2.0, -8.0 * i / N_Q_HEADS).astype(np.float32))


def pool_mixer(u, w_pool, pool_scale):
    b, s, _ = u.shape
    uf = u.astype(jnp.float32).reshape(b, s, N_POOL_GROUPS, POOL_GROUP)
    csum = jnp.pad(jnp.cumsum(uf, axis=1), ((0, 0), (1, 0), (0, 0), (0, 0)))
    t = jnp.arange(s)
    means = []
    for g, w in enumerate(POOL_WINDOWS):
        lo = jnp.clip(t - w // 2, 0, s)
        hi = jnp.clip(t + w // 2, 0, s)
        cnt = (hi - lo).astype(jnp.float32)
        c_g = csum[:, :, g]
        win = jnp.take(c_g, hi, axis=1) - jnp.take(c_g, lo, axis=1)
        means.append(win / cnt[None, :, None])
    mean = jnp.stack(means, axis=2)
    d = (mean - uf).astype(u.dtype)
    y = jnp.einsum('bsgc,gcd->bsgd', d, w_pool)
    return y.reshape(b, s, POOL_WIDTH) * pool_scale


def windowed_gqa(q, k, v, sink):
    b, s = q.shape[0], q.shape[1]
    nb = s // BLOCK
    qb = q.reshape(b, nb, BLOCK, N_KV_HEADS, Q_PER_KV, HEAD_DIM)
    qb = jnp.moveaxis(qb, 1, 0)
    pad = ((0, 0), (BLOCK, BLOCK), (0, 0), (0, 0))
    kp = jnp.pad(k, pad)
    vp = jnp.pad(v, pad)
    slopes = alibi_slopes().reshape(N_KV_HEADS, Q_PER_KV)
    sink_l = sink.astype(jnp.float32).reshape(N_KV_HEADS, Q_PER_KV)
    scale = HEAD_DIM ** -0.5

    def one_block(args):
        i, q_i = args
        k_i = lax.dynamic_slice_in_dim(kp, i * BLOCK, 3 * BLOCK, axis=1)
        v_i = lax.dynamic_slice_in_dim(vp, i * BLOCK, 3 * BLOCK, axis=1)
        qpos = i * BLOCK + jnp.arange(BLOCK)
        kpos = i * BLOCK - BLOCK + jnp.arange(3 * BLOCK)
        dist = jnp.abs(qpos[:, None] - kpos[None, :])
        valid = (dist <= WINDOW) & (kpos >= 0)[None, :] & (kpos < s)[None, :]
        logits = jnp.einsum('bqhgd,bkhd->bhgqk', q_i, k_i,
                            preferred_element_type=jnp.float32) * scale
        logits = logits - slopes[None, :, :, None, None] * dist.astype(jnp.float32)
        logits = jnp.where(valid, logits, NEG_INF)
        sink_col = jnp.broadcast_to(sink_l[None, :, :, None, None], logits.shape[:-1] + (1,))
        probs = jax.nn.softmax(jnp.concatenate([logits, sink_col], axis=-1), axis=-1)[..., :-1]
        out = jnp.einsum('bhgqk,bkhd->bqhgd', probs.astype(v.dtype), v_i)
        return out.reshape(b, BLOCK, ATTN_WIDTH)

    outs = lax.map(one_block, (jnp.arange(nb), qb))
    return jnp.moveaxis(outs, 0, 1).reshape(b, s, ATTN_WIDTH)


def conv_ffn(h, w_gate, w_up, conv_w, conv_b, w_down):
    gate = h @ w_gate
    up = h @ w_up
    gp = jnp.pad(gate, ((0, 0), (1, 1), (0, 0)))
    gate = gp[:, :-2] * conv_w[0] + gp[:, 1:-1] * conv_w[1] + gp[:, 2:] * conv_w[2] + conv_b
    return (jax.nn.gelu(gate, approximate=True) * up) @ w_down


def setup_inputs(seed: int = 0) -> dict:
    key = jax.random.key(seed)
    ks = jax.random.split(key, 16)
    f32 = jnp.float32
    L = DEPTH

    def nrm(k, shape, scale):
        return jax.random.normal(k, shape, f32) * scale

    return {
        "x": jax.random.normal(ks[0], (BATCH, SEQ, D_MODEL), f32),
        "norm_pre_mix": 1.0 + nrm(ks[1], (L, D_MODEL), 0.05),
        "w_in": nrm(ks[2], (L, D_MODEL, IN_WIDTH), D_MODEL ** -0.5),
        "w_pool": nrm(ks[3], (L, N_POOL_GROUPS, POOL_GROUP, POOL_GROUP), POOL_GROUP ** -0.5),
        "pool_scale": 1.0 + nrm(ks[4], (L, POOL_WIDTH), 0.1),
        "attn_sink": nrm(ks[5], (L, N_Q_HEADS), 0.5),
        "w_out": nrm(ks[6], (L, MIX_WIDTH, D_MODEL), MIX_WIDTH ** -0.5),
        "norm_post_mix": 1.0 + nrm(ks[7], (L, D_MODEL), 0.05),
        "norm_pre_ffn": 1.0 + nrm(ks[8], (L, D_MODEL), 0.05),
        "w_gate": nrm(ks[9], (L, D_MODEL, D_FF), D_MODEL ** -0.5),
        "w_up": nrm(ks[10], (L, D_MODEL, D_FF), D_MODEL ** -0.5),
        "conv_w": nrm(ks[11], (L, CONV_WIDTH, D_FF), CONV_WIDTH ** -0.5),
        "conv_b": nrm(ks[12], (L, D_FF), 0.01),
        "w_down": nrm(ks[13], (L, D_FF, D_MODEL), D_FF ** -0.5),
        "norm_post_ffn": 1.0 + nrm(ks[14], (L, D_MODEL), 0.05),
    }


def reference(x, norm_pre_mix, w_in, w_pool, pool_scale, attn_sink, w_out, norm_post_mix,
              norm_pre_ffn, w_gate, w_up, conv_w, conv_b, w_down, norm_post_ffn):
    b, s, _ = x.shape
    for l in range(DEPTH):
        h = rms_norm(x, norm_pre_mix[l])
        p = h @ w_in[l]
        o0 = POOL_WIDTH
        o1 = o0 + ATTN_WIDTH
        o2 = o1 + KV_WIDTH
        u_pool = p[..., :o0]
        q = p[..., o0:o1].reshape(b, s, N_Q_HEADS, HEAD_DIM)
        k = p[..., o1:o2].reshape(b, s, N_KV_HEADS, HEAD_DIM)
        v = p[..., o2:].reshape(b, s, N_KV_HEADS, HEAD_DIM)
        y_pool = pool_mixer(u_pool, w_pool[l], pool_scale[l])
        y_attn = windowed_gqa(q, k, v, attn_sink[l])
        mix = jnp.concatenate([y_pool, y_attn], axis=-1) @ w_out[l]
        x = x + rms_norm(mix, norm_post_mix[l])
        f = rms_norm(x, norm_pre_ffn[l])
        y = conv_ffn(f, w_gate[l], w_up[l], conv_w[l], conv_b[l], w_down[l])
        x = x + rms_norm(y, norm_post_ffn[l])
    return x
```

```python
import functools

import jax
import jax.numpy as jnp
import numpy as np
from jax import lax
from jax.experimental import pallas as pl
from jax.experimental.pallas import tpu as pltpu

F32 = jnp.float32
BF16 = jnp.bfloat16

EPS = 1e-6
NEG_INF = -1e30
POOL_WINDOWS = (2, 4, 8, 16)
HEAD_DIM = 64
Q_PER_KV = 4
WINDOW = 128
QBLK = 128
KWIN = 3 * QBLK
POOL_HALO = 32
CONV_HALO = 16
V7X_VMEM_LIMIT = 60 * 1024 * 1024


def _rms(x, g):
    return x * lax.rsqrt(jnp.mean(x * x, axis=-1, keepdims=True) + EPS) * g


def _inproj_kernel(x_ref, g_ref, w_ref, u_ref, qkv_ref, *, pool_width):
    h = _rms(x_ref[...], g_ref[...]).astype(BF16)
    u_ref[...] = jnp.dot(h, w_ref[:, :pool_width], preferred_element_type=F32)
    qkv_ref[...] = jnp.dot(h, w_ref[:, pool_width:], preferred_element_type=F32).astype(BF16)


def _in_proj(x2, g, w_in, pool_width, tm):
    n, d = x2.shape
    in_width = w_in.shape[1]
    return pl.pallas_call(
        functools.partial(_inproj_kernel, pool_width=pool_width),
        grid=(n // tm,),
        in_specs=[
            pl.BlockSpec((tm, d), lambda i: (i, 0)),
            pl.BlockSpec((1, d), lambda i: (0, 0)),
            pl.BlockSpec((d, in_width), lambda i: (0, 0)),
        ],
        out_specs=[
            pl.BlockSpec((tm, pool_width), lambda i: (i, 0)),
            pl.BlockSpec((tm, in_width - pool_width), lambda i: (i, 0)),
        ],
        out_shape=[
            jax.ShapeDtypeStruct((n, pool_width), F32),
            jax.ShapeDtypeStruct((n, in_width - pool_width), BF16),
        ],
        compiler_params=pltpu.CompilerParams(
            dimension_semantics=("parallel",), vmem_limit_bytes=V7X_VMEM_LIMIT),
        name="in_proj",
    )(x2, g, w_in)


def _pool_kernel(u_ref, up_ref, un_ref, wp_ref, ps_ref, o_ref, b_ref, s1_ref, s2_ref, s4_ref,
                 *, tm, seq, gw):
    i = pl.program_id(1)
    h = POOL_HALO
    b_ref[0:h, :] = jnp.where(i > 0, up_ref[0], 0.0)
    b_ref[h:h + tm, :] = u_ref[0]
    b_ref[h + tm:, :] = jnp.where(i < pl.num_programs(1) - 1, un_ref[0], 0.0)

    n1 = tm + 2 * h - 16
    a1 = b_ref[8:8 + n1, :] + b_ref[7:7 + n1, :]
    s1_ref[8:8 + n1, :] = a1[:, gw:]
    n2 = n1 - 16
    a2 = s1_ref[17:17 + n2, :] + s1_ref[15:15 + n2, :]
    s2_ref[16:16 + n2, :] = a2[:, gw:]
    n4 = n2 - 16
    a4 = s2_ref[26:26 + n4, :] + s2_ref[22:22 + n4, :]
    s4_ref[24:24 + n4, :] = a4[:, gw:]
    a8 = s4_ref[36:36 + tm, :] + s4_ref[28:28 + tm, :]

    wins = (a1[h - 8:h - 8 + tm, :gw], a2[h - 16:h - 16 + tm, :gw], a4[h - 24:h - 24 + tm, :gw], a8)
    t = i * tm + lax.broadcasted_iota(jnp.int32, (tm, 1), 0)
    for g, w in enumerate(POOL_WINDOWS):
        lo = jnp.maximum(t - w // 2, 0)
        hi = jnp.minimum(t + w // 2, seq)
        inv_cnt = 1.0 / (hi - lo).astype(F32)
        cols = slice(g * gw, (g + 1) * gw)
        d = (wins[g] * inv_cnt - b_ref[h:h + tm, cols]).astype(BF16)
        y = jnp.dot(d, wp_ref[g], preferred_element_type=F32) * ps_ref[:, cols]
        o_ref[0, :, cols] = y.astype(BF16)


def _pool(u3, w_pool, pool_scale, tm):
    b, s, pw = u3.shape
    ng = len(POOL_WINDOWS)
    gw = pw // ng
    h = POOL_HALO
    nh = tm // h
    return pl.pallas_call(
        functools.partial(_pool_kernel, tm=tm, seq=s, gw=gw),
        grid=(b, s // tm),
        in_specs=[
            pl.BlockSpec((1, tm, pw), lambda bi, i: (bi, i, 0)),
            pl.BlockSpec((1, h, pw), lambda bi, i: (bi, jnp.maximum(i * nh - 1, 0), 0)),
            pl.BlockSpec((1, h, pw), lambda bi, i: (bi, jnp.minimum((i + 1) * nh, s // h - 1), 0)),
            pl.BlockSpec((ng, gw, gw), lambda bi, i: (0, 0, 0)),
            pl.BlockSpec((1, pw), lambda bi, i: (0, 0)),
        ],
        out_specs=pl.BlockSpec((1, tm, pw), lambda bi, i: (bi, i, 0)),
        out_shape=jax.ShapeDtypeStruct((b, s, pw), BF16),
        scratch_shapes=[
            pltpu.VMEM((tm + 2 * h, pw), F32),
            pltpu.VMEM((tm + 2 * h, pw - gw), F32),
            pltpu.VMEM((tm + 2 * h, pw - 2 * gw), F32),
            pltpu.VMEM((tm + 2 * h, pw - 3 * gw), F32),
        ],
        compiler_params=pltpu.CompilerParams(
            dimension_semantics=("parallel", "parallel"), vmem_limit_bytes=V7X_VMEM_LIMIT),
        name="pool_mixer",
    )(u3, u3, u3, w_pool, pool_scale)


def _attn_kernel(sink_ref, q_ref, k_ref, v_ref, o_ref, *, tq, seq, slopes):
    i = pl.program_id(1)
    n_heads = len(slopes)

    def sub_block(j, carry):
        q0 = pl.multiple_of(i * tq + j * QBLK, QBLK)
        k0 = pl.multiple_of(jnp.clip(q0 - WINDOW, 0, seq - KWIN), QBLK)
        rows = pl.ds(pl.multiple_of(j * QBLK, QBLK), QBLK)
        qpos = q0 + lax.broadcasted_iota(jnp.int32, (QBLK, KWIN), 0)
        kpos = k0 + lax.broadcasted_iota(jnp.int32, (QBLK, KWIN), 1)
        dist_i = jnp.abs(qpos - kpos)
        valid = dist_i <= WINDOW
        dist = dist_i.astype(F32)
        kwin = k_ref[0, pl.ds(k0, KWIN), :]
        vwin = v_ref[0, pl.ds(k0, KWIN), :]
        for hq in range(n_heads):
            hk = hq // Q_PER_KV
            q = q_ref[0, rows, hq * HEAD_DIM:(hq + 1) * HEAD_DIM]
            k = kwin[:, hk * HEAD_DIM:(hk + 1) * HEAD_DIM]
            v = vwin[:, hk * HEAD_DIM:(hk + 1) * HEAD_DIM]
            s = lax.dot_general(q, k, (((1,), (1,)), ((), ())), preferred_element_type=F32)
            logits = jnp.where(valid, s * (HEAD_DIM ** -0.5) - slopes[hq] * dist, NEG_INF)
            sink = sink_ref[hq]
            m = jnp.maximum(jnp.max(logits, axis=-1, keepdims=True), sink)
            e = jnp.exp(logits - m)
            den = jnp.sum(e, axis=-1, keepdims=True) + jnp.exp(sink - m)
            o = jnp.dot(e.astype(BF16), v, preferred_element_type=F32) * (1.0 / den)
            o_ref[0, rows, hq * HEAD_DIM:(hq + 1) * HEAD_DIM] = o.astype(BF16)
        return carry

    lax.fori_loop(0, tq // QBLK, sub_block, 0)


def _attention(qkv3, sink, attn_width, kv_width, tq):
    b, s, _ = qkv3.shape
    n_heads = attn_width // HEAD_DIM
    idx = np.arange(1, n_heads + 1, dtype=np.float32)
    slopes = tuple(float(v) for v in np.power(2.0, -8.0 * idx / n_heads).astype(np.float32))
    kblk = attn_width // kv_width
    return pl.pallas_call(
        functools.partial(_attn_kernel, tq=tq, seq=s, slopes=slopes),
        grid=(b, s // tq),
        in_specs=[
            pl.BlockSpec(memory_space=pltpu.SMEM),
            pl.BlockSpec((1, tq, attn_width), lambda bi, i: (bi, i, 0)),
            pl.BlockSpec((1, s, kv_width), lambda bi, i: (bi, 0, kblk)),
            pl.BlockSpec((1, s, kv_width), lambda bi, i: (bi, 0, kblk + 1)),
        ],
        out_specs=pl.BlockSpec((1, tq, attn_width), lambda bi, i: (bi, i, 0)),
        out_shape=jax.ShapeDtypeStruct((b, s, attn_width), BF16),
        compiler_params=pltpu.CompilerParams(
            dimension_semantics=("parallel", "arbitrary"), vmem_limit_bytes=V7X_VMEM_LIMIT),
        name="window_attn",
    )(sink, qkv3, qkv3, qkv3)


def _outproj_kernel(yp_ref, ya_ref, w_ref, x_ref, g_ref, o_ref, *, pool_width):
    mix = jnp.dot(yp_ref[...], w_ref[:pool_width, :], preferred_element_type=F32)
    mix = mix + jnp.dot(ya_ref[...], w_ref[pool_width:, :], preferred_element_type=F32)
    o_ref[...] = x_ref[...] + _rms(mix, g_ref[...])


def _out_proj(yp, ya, w_out, x2, g, tm):
    n, d = x2.shape
    pw = yp.shape[1]
    aw = ya.shape[1]
    return pl.pallas_call(
        functools.partial(_outproj_kernel, pool_width=pw),
        grid=(n // tm,),
        in_specs=[
            pl.BlockSpec((tm, pw), lambda i: (i, 0)),
            pl.BlockSpec((tm, aw), lambda i: (i, 0)),
            pl.BlockSpec((pw + aw, d), lambda i: (0, 0)),
            pl.BlockSpec((tm, d), lambda i: (i, 0)),
            pl.BlockSpec((1, d), lambda i: (0, 0)),
        ],
        out_specs=pl.BlockSpec((tm, d), lambda i: (i, 0)),
        out_shape=jax.ShapeDtypeStruct((n, d), F32),
        compiler_params=pltpu.CompilerParams(
            dimension_semantics=("parallel",), vmem_limit_bytes=V7X_VMEM_LIMIT),
        name="out_proj",
    )(yp, ya, w_out, x2, g)


def _gelu_tanh(x):
    return 0.5 * x * (1.0 + jnp.tanh(np.sqrt(2.0 / np.pi).astype(np.float32) * (x + 0.044715 * (x * x * x))))


def _ffn_kernel(x_ref, xp_ref, xn_ref, gpre_ref, wg_ref, wu_ref, cw_ref, cb_ref, wd_ref, gpost_ref,
                o_ref, f_ref, g_ref, *, tm, tiles_per_seq):
    i = pl.program_id(0)
    c = pl.program_id(1)
    hal = CONV_HALO

    @pl.when(c == 0)
    def _():
        it = i % tiles_per_seq
        gp = gpre_ref[...]
        f_ref[hal:hal + tm, :] = _rms(x_ref[...], gp).astype(BF16)
        fp = _rms(xp_ref[...], gp)
        fn = _rms(xn_ref[...], gp)
        row = lax.broadcasted_iota(jnp.int32, (hal, 1), 0)
        fp16 = jnp.concatenate([jnp.zeros_like(fp), fp], axis=0)
        fn16 = jnp.concatenate([fn, jnp.zeros_like(fn)], axis=0)
        prev_row = jnp.where(it > 0, hal - 1, -1)
        next_row = jnp.where(it < tiles_per_seq - 1, 0, -1)
        f_ref[0:hal, :] = jnp.where(row == prev_row, fp16, 0.0).astype(BF16)
        f_ref[hal + tm:, :] = jnp.where(row == next_row, fn16, 0.0).astype(BF16)

    g_ref[...] = jnp.dot(f_ref[...], wg_ref[...], preferred_element_type=F32)
    up = jnp.dot(f_ref[hal:hal + tm, :], wu_ref[...], preferred_element_type=F32)
    cw = cw_ref[...]
    gate = (g_ref[hal - 1:hal - 1 + tm, :] * cw[0:1, :] + g_ref[hal:hal + tm, :] * cw[1:2, :]
            + g_ref[hal + 1:hal + 1 + tm, :] * cw[2:3, :] + cb_ref[...])
    hmid = (_gelu_tanh(gate) * up).astype(BF16)
    y = jnp.dot(hmid, wd_ref[...], preferred_element_type=F32)

    @pl.when(c == 0)
    def _():
        o_ref[...] = y

    @pl.when(c > 0)
    def _():
        o_ref[...] += y

    @pl.when(c == pl.num_programs(1) - 1)
    def _():
        o_ref[...] = x_ref[...] + _rms(o_ref[...], gpost_ref[...])


def _ffn(x1, g_pre, w_gate, w_up, conv_w, conv_b, w_down, g_post, seq, tm, tc):
    n, d = x1.shape
    dff = w_gate.shape[1]
    hal = CONV_HALO
    nb8 = tm // 8
    tiles_per_seq = seq // tm
    return pl.pallas_call(
        functools.partial(_ffn_kernel, tm=tm, tiles_per_seq=tiles_per_seq),
        grid=(n // tm, dff // tc),
        in_specs=[
            pl.BlockSpec((tm, d), lambda i, c: (i, 0)),
            pl.BlockSpec((8, d), lambda i, c: (jnp.maximum(i * nb8 - 1, 0), 0)),
            pl.BlockSpec((8, d), lambda i, c: (jnp.minimum((i + 1) * nb8, n // 8 - 1), 0)),
            pl.BlockSpec((1, d), lambda i, c: (0, 0)),
            pl.BlockSpec((d, tc), lambda i, c: (0, c)),
            pl.BlockSpec((d, tc), lambda i, c: (0, c)),
            pl.BlockSpec((3, tc), lambda i, c: (0, c)),
            pl.BlockSpec((1, tc), lambda i, c: (0, c)),
            pl.BlockSpec((tc, d), lambda i, c: (c, 0)),
            pl.BlockSpec((1, d), lambda i, c: (0, 0)),
        ],
        out_specs=pl.BlockSpec((tm, d), lambda i, c: (i, 0)),
        out_shape=jax.ShapeDtypeStruct((n, d), F32),
        scratch_shapes=[
            pltpu.VMEM((tm + 2 * hal, d), BF16),
            pltpu.VMEM((tm + 2 * hal, tc), F32),
        ],
        compiler_params=pltpu.CompilerParams(
            dimension_semantics=("parallel", "arbitrary"), vmem_limit_bytes=V7X_VMEM_LIMIT),
        name="conv_ffn",
    )(x1, x1, x1, g_pre, w_gate, w_up, conv_w, conv_b, w_down, g_post)


def kernel(x, norm_pre_mix, w_in, w_pool, pool_scale, attn_sink, w_out, norm_post_mix, norm_pre_ffn,
           w_gate, w_up, conv_w, conv_b, w_down, norm_post_ffn):
    b, s, d = x.shape
    depth = w_in.shape[0]
    pool_width = pool_scale.shape[1]
    n_heads = attn_sink.shape[1]
    attn_width = n_heads * HEAD_DIM
    kv_width = (w_in.shape[2] - pool_width - attn_width) // 2
    x2 = x.reshape(b * s, d)
    for l in range(depth):
        u, qkv = _in_proj(x2, norm_pre_mix[l][None, :], w_in[l].astype(BF16), pool_width, tm=512)
        y_pool = _pool(u.reshape(b, s, pool_width), w_pool[l].astype(BF16), pool_scale[l][None, :], tm=512)
        y_attn = _attention(qkv.reshape(b, s, -1), attn_sink[l], attn_width, kv_width, tq=512)
        x2 = _out_proj(y_pool.reshape(b * s, pool_width), y_attn.reshape(b * s, attn_width),
                       w_out[l].astype(BF16), x2, norm_post_mix[l][None, :], tm=512)
        x2 = _ffn(x2, norm_pre_ffn[l][None, :], w_gate[l].astype(BF16), w_up[l].astype(BF16), conv_w[l],
                  conv_b[l][None, :], w_down[l].astype(BF16), norm_post_ffn[l][None, :], seq=s, tm=512, tc=512)
    return x2.reshape(b, s, d)
```

```python
import functools

import jax
import jax.numpy as jnp
import numpy as np
from jax import lax
from jax.experimental import pallas as pl
from jax.experimental.pallas import tpu as pltpu

F32 = jnp.float32
BF16 = jnp.bfloat16

EPS = 1e-6
NEG_INF = -1e30
POOL_WINDOWS = (2, 4, 8, 16)
HEAD_DIM = 64
Q_PER_KV = 4
WINDOW = 128
QBLK = 128
KWIN = 3 * QBLK
POOL_HALO = 32
CONV_HALO = 16
V7X_VMEM_LIMIT = 60 * 1024 * 1024


def _rms(x, g):
    return x * lax.rsqrt(jnp.mean(x * x, axis=-1, keepdims=True) + EPS) * g


def _inproj_kernel(x_ref, g_ref, w_ref, u_ref, qkv_ref, *, pool_width, attn_width):
    h = _rms(x_ref[...], g_ref[...]).astype(BF16)
    u_ref[...] = jnp.dot(h, w_ref[:, :pool_width], preferred_element_type=F32)
    q = jnp.dot(h, w_ref[:, pool_width:pool_width + attn_width], preferred_element_type=F32)
    qkv_ref[:, :attn_width] = (q * (HEAD_DIM ** -0.5)).astype(BF16)
    kv = jnp.dot(h, w_ref[:, pool_width + attn_width:], preferred_element_type=F32)
    qkv_ref[:, attn_width:] = kv.astype(BF16)


def _in_proj(x2, g, w_in, pool_width, attn_width, tm):
    n, d = x2.shape
    in_width = w_in.shape[1]
    return pl.pallas_call(
        functools.partial(_inproj_kernel, pool_width=pool_width, attn_width=attn_width),
        grid=(n // tm,),
        in_specs=[
            pl.BlockSpec((tm, d), lambda i: (i, 0)),
            pl.BlockSpec((1, d), lambda i: (0, 0)),
            pl.BlockSpec((d, in_width), lambda i: (0, 0)),
        ],
        out_specs=[
            pl.BlockSpec((tm, pool_width), lambda i: (i, 0)),
            pl.BlockSpec((tm, in_width - pool_width), lambda i: (i, 0)),
        ],
        out_shape=[
            jax.ShapeDtypeStruct((n, pool_width), F32),
            jax.ShapeDtypeStruct((n, in_width - pool_width), BF16),
        ],
        compiler_params=pltpu.CompilerParams(
            dimension_semantics=("parallel",), vmem_limit_bytes=V7X_VMEM_LIMIT),
        name="in_proj",
    )(x2, g, w_in)


def _pool_kernel(u_ref, up_ref, un_ref, wp_ref, ps_ref, o_ref, b_ref, s1_ref, s2_ref, s4_ref,
                 *, tm, seq, gw):
    i = pl.program_id(1)
    h = POOL_HALO
    b_ref[0:h, :] = jnp.where(i > 0, up_ref[0], 0.0)
    b_ref[h:h + tm, :] = u_ref[0]
    b_ref[h + tm:, :] = jnp.where(i < pl.num_programs(1) - 1, un_ref[0], 0.0)

    n1 = tm + 2 * h - 16
    a1 = b_ref[8:8 + n1, :] + b_ref[7:7 + n1, :]
    s1_ref[8:8 + n1, :] = a1[:, gw:]
    n2 = n1 - 16
    a2 = s1_ref[17:17 + n2, :] + s1_ref[15:15 + n2, :]
    s2_ref[16:16 + n2, :] = a2[:, gw:]
    n4 = n2 - 16
    a4 = s2_ref[26:26 + n4, :] + s2_ref[22:22 + n4, :]
    s4_ref[24:24 + n4, :] = a4[:, gw:]
    a8 = s4_ref[36:36 + tm, :] + s4_ref[28:28 + tm, :]

    wins = (a1[h - 8:h - 8 + tm, :gw], a2[h - 16:h - 16 + tm, :gw], a4[h - 24:h - 24 + tm, :gw], a8)
    t = i * tm + lax.broadcasted_iota(jnp.int32, (tm, 1), 0)
    for g, w in enumerate(POOL_WINDOWS):
        lo = jnp.maximum(t - w // 2, 0)
        hi = jnp.minimum(t + w // 2, seq)
        inv_cnt = 1.0 / (hi - lo).astype(F32)
        cols = slice(g * gw, (g + 1) * gw)
        d = (wins[g] * inv_cnt - b_ref[h:h + tm, cols]).astype(BF16)
        y = jnp.dot(d, wp_ref[g], preferred_element_type=F32) * ps_ref[:, cols]
        o_ref[0, :, cols] = y.astype(BF16)


def _pool(u3, w_pool, pool_scale, tm):
    b, s, pw = u3.shape
    ng = len(POOL_WINDOWS)
    gw = pw // ng
    h = POOL_HALO
    nh = tm // h
    return pl.pallas_call(
        functools.partial(_pool_kernel, tm=tm, seq=s, gw=gw),
        grid=(b, s // tm),
        in_specs=[
            pl.BlockSpec((1, tm, pw), lambda bi, i: (bi, i, 0)),
            pl.BlockSpec((1, h, pw), lambda bi, i: (bi, jnp.maximum(i * nh - 1, 0), 0)),
            pl.BlockSpec((1, h, pw), lambda bi, i: (bi, jnp.minimum((i + 1) * nh, s // h - 1), 0)),
            pl.BlockSpec((ng, gw, gw), lambda bi, i: (0, 0, 0)),
            pl.BlockSpec((1, pw), lambda bi, i: (0, 0)),
        ],
        out_specs=pl.BlockSpec((1, tm, pw), lambda bi, i: (bi, i, 0)),
        out_shape=jax.ShapeDtypeStruct((b, s, pw), BF16),
        scratch_shapes=[
            pltpu.VMEM((tm + 2 * h, pw), F32),
            pltpu.VMEM((tm + 2 * h, pw - gw), F32),
            pltpu.VMEM((tm + 2 * h, pw - 2 * gw), F32),
            pltpu.VMEM((tm + 2 * h, pw - 3 * gw), F32),
        ],
        compiler_params=pltpu.CompilerParams(
            dimension_semantics=("parallel", "parallel"), vmem_limit_bytes=V7X_VMEM_LIMIT),
        name="pool_mixer",
    )(u3, u3, u3, w_pool, pool_scale)


def _attn_kernel(sink_ref, q_ref, k_ref, v_ref, o_ref, bias_ref, sinkrow_ref, vt_ref, *, tq, seq, slopes):
    bi = pl.program_id(0)
    i = pl.program_id(1)
    grp = Q_PER_KV
    n_heads = len(slopes)
    n_kv = n_heads // grp

    @pl.when((bi == 0) & (i == 0))
    def _():
        kj = lax.broadcasted_iota(jnp.int32, (KWIN, QBLK), 0)
        qi = lax.broadcasted_iota(jnp.int32, (KWIN, QBLK), 1)
        for v in range(3):
            dist_i = jnp.abs(v * QBLK + qi - kj)
            valid = dist_i <= WINDOW
            dist = dist_i.astype(F32)
            for h in range(n_heads):
                bias_ref[v, h] = jnp.where(valid, -(slopes[h] * dist), -jnp.inf)
        for h in range(n_heads):
            sinkrow_ref[h] = jnp.full((1, QBLK), sink_ref[h], F32)

    @pl.when(i == 0)
    def _():
        def xpose(cb, carry):
            blk = pl.ds(pl.multiple_of(cb * QBLK, QBLK), QBLK)
            vt_ref[:, blk] = v_ref[0, blk, :].T
            return carry
        lax.fori_loop(0, seq // QBLK, xpose, 0)

    def sub_block(j, carry):
        q0 = i * tq + j * QBLK
        k0 = pl.multiple_of(jnp.clip(q0 - WINDOW, 0, seq - KWIN), QBLK)
        variant = (q0 - k0) // QBLK
        rows = pl.ds(pl.multiple_of(j * QBLK, QBLK), QBLK)
        for hk in range(n_kv):
            q4 = q_ref[0, rows, hk * grp * HEAD_DIM:(hk + 1) * grp * HEAD_DIM]
            q = jnp.concatenate([q4[:, g * HEAD_DIM:(g + 1) * HEAD_DIM] for g in range(grp)], axis=0)
            k = k_ref[0, pl.ds(k0, KWIN), hk * HEAD_DIM:(hk + 1) * HEAD_DIM]
            st = lax.dot_general(k, q, (((1,), (1,)), ((), ())), preferred_element_type=F32)
            e_cols, inv_cols = [], []
            for g in range(grp):
                h = hk * grp + g
                logits = st[:, g * QBLK:(g + 1) * QBLK] + bias_ref[variant, h]
                sink = sinkrow_ref[h]
                m = jnp.maximum(jnp.max(logits, axis=0, keepdims=True), sink)
                e = jnp.exp(logits - m)
                den = jnp.sum(e, axis=0, keepdims=True) + jnp.exp(sink - m)
                e_cols.append(e.astype(BF16))
                inv_cols.append(1.0 / den)
            et = jnp.concatenate(e_cols, axis=1)
            vt = vt_ref[hk * HEAD_DIM:(hk + 1) * HEAD_DIM, pl.ds(k0, KWIN)]
            ot = jnp.dot(vt, et, preferred_element_type=F32) * jnp.concatenate(inv_cols, axis=1)
            o4 = jnp.concatenate([ot[:, g * QBLK:(g + 1) * QBLK].T for g in range(grp)], axis=1)
            o_ref[0, rows, hk * grp * HEAD_DIM:(hk + 1) * grp * HEAD_DIM] = o4.astype(BF16)
        return carry

    lax.fori_loop(0, tq // QBLK, sub_block, 0)


def _attention(qkv3, sink, attn_width, kv_width, tq):
    b, s, _ = qkv3.shape
    n_heads = attn_width // HEAD_DIM
    n_kv = n_heads // Q_PER_KV
    idx = np.arange(1, n_heads + 1, dtype=np.float32)
    slopes = tuple(float(v) for v in np.power(2.0, -8.0 * idx / n_heads).astype(np.float32))
    kblk = attn_width // kv_width
    return pl.pallas_call(
        functools.partial(_attn_kernel, tq=tq, seq=s, slopes=slopes),
        grid=(b, s // tq),
        in_specs=[
            pl.BlockSpec(memory_space=pltpu.SMEM),
            pl.BlockSpec((1, tq, attn_width), lambda bi, i: (bi, i, 0)),
            pl.BlockSpec((1, s, kv_width), lambda bi, i: (bi, 0, kblk)),
            pl.BlockSpec((1, s, kv_width), lambda bi, i: (bi, 0, kblk + 1)),
        ],
        out_specs=pl.BlockSpec((1, tq, attn_width), lambda bi, i: (bi, i, 0)),
        out_shape=jax.ShapeDtypeStruct((b, s, attn_width), BF16),
        scratch_shapes=[
            pltpu.VMEM((3, n_heads, KWIN, QBLK), F32),
            pltpu.VMEM((n_heads, 1, QBLK), F32),
            pltpu.VMEM((kv_width, s), BF16),
        ],
        compiler_params=pltpu.CompilerParams(
            dimension_semantics=("arbitrary", "arbitrary"), vmem_limit_bytes=V7X_VMEM_LIMIT),
        name="window_attn",
    )(sink, qkv3, qkv3, qkv3)


def _outproj_kernel(yp_ref, ya_ref, w_ref, x_ref, g_ref, o_ref, *, pool_width):
    mix = jnp.dot(yp_ref[...], w_ref[:pool_width, :], preferred_element_type=F32)
    mix = mix + jnp.dot(ya_ref[...], w_ref[pool_width:, :], preferred_element_type=F32)
    o_ref[...] = x_ref[...] + _rms(mix, g_ref[...])


def _out_proj(yp, ya, w_out, x2, g, tm):
    n, d = x2.shape
    pw = yp.shape[1]
    aw = ya.shape[1]
    return pl.pallas_call(
        functools.partial(_outproj_kernel, pool_width=pw),
        grid=(n // tm,),
        in_specs=[
            pl.BlockSpec((tm, pw), lambda i: (i, 0)),
            pl.BlockSpec((tm, aw), lambda i: (i, 0)),
            pl.BlockSpec((pw + aw, d), lambda i: (0, 0)),
            pl.BlockSpec((tm, d), lambda i: (i, 0)),
            pl.BlockSpec((1, d), lambda i: (0, 0)),
        ],
        out_specs=pl.BlockSpec((tm, d), lambda i: (i, 0)),
        out_shape=jax.ShapeDtypeStruct((n, d), F32),
        compiler_params=pltpu.CompilerParams(
            dimension_semantics=("parallel",), vmem_limit_bytes=V7X_VMEM_LIMIT),
        name="out_proj",
    )(yp, ya, w_out, x2, g)


def _gelu_tanh(x):
    return 0.5 * x * (1.0 + jnp.tanh(np.sqrt(2.0 / np.pi).astype(np.float32) * (x + 0.044715 * (x * x * x))))


def _ffn_kernel(x_ref, xp_ref, xn_ref, gpre_ref, wg_ref, wu_ref, cw_ref, cb_ref, wd_ref, gpost_ref,
                o_ref, f_ref, g_ref, *, tm, tiles_per_seq):
    i = pl.program_id(0)
    c = pl.program_id(1)
    hal = CONV_HALO

    @pl.when(c == 0)
    def _():
        it = i % tiles_per_seq
        gp = gpre_ref[...]
        f_ref[hal:hal + tm, :] = _rms(x_ref[...], gp).astype(BF16)
        fp = _rms(xp_ref[...], gp)
        fn = _rms(xn_ref[...], gp)
        row = lax.broadcasted_iota(jnp.int32, (hal, 1), 0)
        fp16 = jnp.concatenate([jnp.zeros_like(fp), fp], axis=0)
        fn16 = jnp.concatenate([fn, jnp.zeros_like(fn)], axis=0)
        prev_row = jnp.where(it > 0, hal - 1, -1)
        next_row = jnp.where(it < tiles_per_seq - 1, 0, -1)
        f_ref[0:hal, :] = jnp.where(row == prev_row, fp16, 0.0).astype(BF16)
        f_ref[hal + tm:, :] = jnp.where(row == next_row, fn16, 0.0).astype(BF16)

    g_ref[...] = jnp.dot(f_ref[...], wg_ref[...], preferred_element_type=F32)
    up = jnp.dot(f_ref[hal:hal + tm, :], wu_ref[...], preferred_element_type=F32)
    cw = cw_ref[...]
    gate = (g_ref[hal - 1:hal - 1 + tm, :] * cw[0:1, :] + g_ref[hal:hal + tm, :] * cw[1:2, :]
            + g_ref[hal + 1:hal + 1 + tm, :] * cw[2:3, :] + cb_ref[...])
    hmid = (_gelu_tanh(gate) * up).astype(BF16)
    y = jnp.dot(hmid, wd_ref[...], preferred_element_type=F32)

    @pl.when(c == 0)
    def _():
        o_ref[...] = y

    @pl.when(c > 0)
    def _():
        o_ref[...] += y

    @pl.when(c == pl.num_programs(1) - 1)
    def _():
        o_ref[...] = x_ref[...] + _rms(o_ref[...], gpost_ref[...])


def _ffn(x1, g_pre, w_gate, w_up, conv_w, conv_b, w_down, g_post, seq, tm, tc):
    n, d = x1.shape
    dff = w_gate.shape[1]
    hal = CONV_HALO
    nb8 = tm // 8
    tiles_per_seq = seq // tm
    return pl.pallas_call(
        functools.partial(_ffn_kernel, tm=tm, tiles_per_seq=tiles_per_seq),
        grid=(n // tm, dff // tc),
        in_specs=[
            pl.BlockSpec((tm, d), lambda i, c: (i, 0)),
            pl.BlockSpec((8, d), lambda i, c: (jnp.maximum(i * nb8 - 1, 0), 0)),
            pl.BlockSpec((8, d), lambda i, c: (jnp.minimum((i + 1) * nb8, n // 8 - 1), 0)),
            pl.BlockSpec((1, d), lambda i, c: (0, 0)),
            pl.BlockSpec((d, tc), lambda i, c: (0, c)),
            pl.BlockSpec((d, tc), lambda i, c: (0, c)),
            pl.BlockSpec((3, tc), lambda i, c: (0, c)),
            pl.BlockSpec((1, tc), lambda i, c: (0, c)),
            pl.BlockSpec((tc, d), lambda i, c: (c, 0)),
            pl.BlockSpec((1, d), lambda i, c: (0, 0)),
        ],
        out_specs=pl.BlockSpec((tm, d), lambda i, c: (i, 0)),
        out_shape=jax.ShapeDtypeStruct((n, d), F32),
        scratch_shapes=[
            pltpu.VMEM((tm + 2 * hal, d), BF16),
            pltpu.VMEM((tm + 2 * hal, tc), F32),
        ],
        compiler_params=pltpu.CompilerParams(
            dimension_semantics=("parallel", "arbitrary"), vmem_limit_bytes=V7X_VMEM_LIMIT),
        name="conv_ffn",
    )(x1, x1, x1, g_pre, w_gate, w_up, conv_w, conv_b, w_down, g_post)


def kernel(x, norm_pre_mix, w_in, w_pool, pool_scale, attn_sink, w_out, norm_post_mix, norm_pre_ffn,
           w_gate, w_up, conv_w, conv_b, w_down, norm_post_ffn):
    b, s, d = x.shape
    depth = w_in.shape[0]
    pool_width = pool_scale.shape[1]
    n_heads = attn_sink.shape[1]
    attn_width = n_heads * HEAD_DIM
    kv_width = (w_in.shape[2] - pool_width - attn_width) // 2
    x2 = x.reshape(b * s, d)
    for l in range(depth):
        u, qkv = _in_proj(x2, norm_pre_mix[l][None, :], w_in[l].astype(BF16), pool_width, attn_width, tm=512)
        y_pool = _pool(u.reshape(b, s, pool_width), w_pool[l].astype(BF16), pool_scale[l][None, :], tm=512)
        y_attn = _attention(qkv.reshape(b, s, -1), attn_sink[l], attn_width, kv_width, tq=512)
        x2 = _out_proj(y_pool.reshape(b * s, pool_width), y_attn.reshape(b * s, attn_width),
                       w_out[l].astype(BF16), x2, norm_post_mix[l][None, :], tm=512)
        x2 = _ffn(x2, norm_pre_ffn[l][None, :], w_gate[l].astype(BF16), w_up[l].astype(BF16), conv_w[l],
                  conv_b[l][None, :], w_down[l].astype(BF16), norm_post_ffn[l][None, :], seq=s, tm=512, tc=512)
    return x2.reshape(b, s, d)
```

```python
import functools

import jax
import jax.numpy as jnp
import numpy as np
from jax import lax
from jax.experimental import pallas as pl
from jax.experimental.pallas import tpu as pltpu

F32 = jnp.float32
BF16 = jnp.bfloat16

EPS = 1e-6
NEG_INF = -1e30
POOL_WINDOWS = (2, 4, 8, 16)
HEAD_DIM = 64
Q_PER_KV = 4
WINDOW = 128
QBLK = 128
KWIN = 3 * QBLK
POOL_HALO = 32
CONV_HALO = 16
FFN_CHUNK = 512
V7X_VMEM_LIMIT = 62 * 1024 * 1024


def _rms(x, g):
    return x * lax.rsqrt(jnp.mean(x * x, axis=-1, keepdims=True) + EPS) * g


def _inproj_kernel(x_ref, g_ref, w_ref, u_ref, qkv_ref, *, pool_width, attn_width):
    h = _rms(x_ref[...], g_ref[...]).astype(BF16)
    u_ref[...] = jnp.dot(h, w_ref[:, :pool_width], preferred_element_type=F32)
    q = jnp.dot(h, w_ref[:, pool_width:pool_width + attn_width], preferred_element_type=F32)
    qkv_ref[:, :attn_width] = (q * (HEAD_DIM ** -0.5)).astype(BF16)
    kv = jnp.dot(h, w_ref[:, pool_width + attn_width:], preferred_element_type=F32)
    qkv_ref[:, attn_width:] = kv.astype(BF16)


def _in_proj(x2, g, w_in, pool_width, attn_width, tm):
    n, d = x2.shape
    in_width = w_in.shape[1]
    return pl.pallas_call(
        functools.partial(_inproj_kernel, pool_width=pool_width, attn_width=attn_width),
        grid=(n // tm,),
        in_specs=[
            pl.BlockSpec((tm, d), lambda i: (i, 0)),
            pl.BlockSpec((1, d), lambda i: (0, 0)),
            pl.BlockSpec((d, in_width), lambda i: (0, 0)),
        ],
        out_specs=[
            pl.BlockSpec((tm, pool_width), lambda i: (i, 0)),
            pl.BlockSpec((tm, in_width - pool_width), lambda i: (i, 0)),
        ],
        out_shape=[
            jax.ShapeDtypeStruct((n, pool_width), F32),
            jax.ShapeDtypeStruct((n, in_width - pool_width), BF16),
        ],
        compiler_params=pltpu.CompilerParams(
            dimension_semantics=("parallel",), vmem_limit_bytes=V7X_VMEM_LIMIT),
        name="in_proj",
    )(x2, g, w_in)


def _pool_kernel(u_ref, up_ref, un_ref, wp_ref, ps_ref, o_ref, b_ref, s1_ref, s2_ref, s4_ref,
                 *, tm, seq, gw):
    i = pl.program_id(1)
    h = POOL_HALO
    b_ref[0:h, :] = jnp.where(i > 0, up_ref[0], 0.0)
    b_ref[h:h + tm, :] = u_ref[0]
    b_ref[h + tm:, :] = jnp.where(i < pl.num_programs(1) - 1, un_ref[0], 0.0)

    n1 = tm + 2 * h - 16
    a1 = b_ref[8:8 + n1, :] + b_ref[7:7 + n1, :]
    s1_ref[8:8 + n1, :] = a1[:, gw:]
    n2 = n1 - 16
    a2 = s1_ref[17:17 + n2, :] + s1_ref[15:15 + n2, :]
    s2_ref[16:16 + n2, :] = a2[:, gw:]
    n4 = n2 - 16
    a4 = s2_ref[26:26 + n4, :] + s2_ref[22:22 + n4, :]
    s4_ref[24:24 + n4, :] = a4[:, gw:]
    a8 = s4_ref[36:36 + tm, :] + s4_ref[28:28 + tm, :]

    wins = (a1[h - 8:h - 8 + tm, :gw], a2[h - 16:h - 16 + tm, :gw], a4[h - 24:h - 24 + tm, :gw], a8)
    t = i * tm + lax.broadcasted_iota(jnp.int32, (tm, 1), 0)
    for g, w in enumerate(POOL_WINDOWS):
        lo = jnp.maximum(t - w // 2, 0)
        hi = jnp.minimum(t + w // 2, seq)
        inv_cnt = 1.0 / (hi - lo).astype(F32)
        cols = slice(g * gw, (g + 1) * gw)
        d = (wins[g] * inv_cnt - b_ref[h:h + tm, cols]).astype(BF16)
        y = jnp.dot(d, wp_ref[g], preferred_element_type=F32) * ps_ref[:, cols]
        o_ref[0, :, cols] = y.astype(BF16)


def _pool(u3, w_pool, pool_scale, tm):
    b, s, pw = u3.shape
    ng = len(POOL_WINDOWS)
    gw = pw // ng
    h = POOL_HALO
    nh = tm // h
    return pl.pallas_call(
        functools.partial(_pool_kernel, tm=tm, seq=s, gw=gw),
        grid=(b, s // tm),
        in_specs=[
            pl.BlockSpec((1, tm, pw), lambda bi, i: (bi, i, 0)),
            pl.BlockSpec((1, h, pw), lambda bi, i: (bi, jnp.maximum(i * nh - 1, 0), 0)),
            pl.BlockSpec((1, h, pw), lambda bi, i: (bi, jnp.minimum((i + 1) * nh, s // h - 1), 0)),
            pl.BlockSpec((ng, gw, gw), lambda bi, i: (0, 0, 0)),
            pl.BlockSpec((1, pw), lambda bi, i: (0, 0)),
        ],
        out_specs=pl.BlockSpec((1, tm, pw), lambda bi, i: (bi, i, 0)),
        out_shape=jax.ShapeDtypeStruct((b, s, pw), BF16),
        scratch_shapes=[
            pltpu.VMEM((tm + 2 * h, pw), F32),
            pltpu.VMEM((tm + 2 * h, pw - gw), F32),
            pltpu.VMEM((tm + 2 * h, pw - 2 * gw), F32),
            pltpu.VMEM((tm + 2 * h, pw - 3 * gw), F32),
        ],
        compiler_params=pltpu.CompilerParams(
            dimension_semantics=("parallel", "parallel"), vmem_limit_bytes=V7X_VMEM_LIMIT),
        name="pool_mixer",
    )(u3, u3, u3, w_pool, pool_scale)


def _attn_kernel(sink_ref, q_ref, k_ref, v_ref, o_ref, bias_ref, sinkrow_ref, vt_ref, *, tq, seq, slopes):
    bi = pl.program_id(0)
    i = pl.program_id(1)
    grp = Q_PER_KV
    n_heads = len(slopes)
    n_kv = n_heads // grp

    @pl.when((bi == 0) & (i == 0))
    def _():
        kj = lax.broadcasted_iota(jnp.int32, (KWIN, QBLK), 0)
        qi = lax.broadcasted_iota(jnp.int32, (KWIN, QBLK), 1)
        for v in range(3):
            dist_i = jnp.abs(v * QBLK + qi - kj)
            valid = dist_i <= WINDOW
            dist = dist_i.astype(F32)
            for h in range(n_heads):
                bias_ref[v, h] = jnp.where(valid, -(slopes[h] * dist), -jnp.inf)
        for h in range(n_heads):
            sinkrow_ref[h] = jnp.full((1, QBLK), sink_ref[h], F32)

    @pl.when(i == 0)
    def _():
        def xpose(cb, carry):
            blk = pl.ds(pl.multiple_of(cb * QBLK, QBLK), QBLK)
            vt_ref[:, blk] = v_ref[0, blk, :].T
            return carry
        lax.fori_loop(0, seq // QBLK, xpose, 0)

    def sub_block(j, carry):
        q0 = i * tq + j * QBLK
        k0 = pl.multiple_of(jnp.clip(q0 - WINDOW, 0, seq - KWIN), QBLK)
        variant = (q0 - k0) // QBLK
        rows = pl.ds(pl.multiple_of(j * QBLK, QBLK), QBLK)
        for hk in range(n_kv):
            q4 = q_ref[0, rows, hk * grp * HEAD_DIM:(hk + 1) * grp * HEAD_DIM]
            q = jnp.concatenate([q4[:, g * HEAD_DIM:(g + 1) * HEAD_DIM] for g in range(grp)], axis=0)
            k = k_ref[0, pl.ds(k0, KWIN), hk * HEAD_DIM:(hk + 1) * HEAD_DIM]
            st = lax.dot_general(k, q, (((1,), (1,)), ((), ())), preferred_element_type=F32)
            e_cols, inv_cols = [], []
            for g in range(grp):
                h = hk * grp + g
                logits = st[:, g * QBLK:(g + 1) * QBLK] + bias_ref[variant, h]
                sink = sinkrow_ref[h]
                m = jnp.maximum(jnp.max(logits, axis=0, keepdims=True), sink)
                e = jnp.exp(logits - m)
                den = jnp.sum(e, axis=0, keepdims=True) + jnp.exp(sink - m)
                e_cols.append(e.astype(BF16))
                inv_cols.append(1.0 / den)
            et = jnp.concatenate(e_cols, axis=1)
            vt = vt_ref[hk * HEAD_DIM:(hk + 1) * HEAD_DIM, pl.ds(k0, KWIN)]
            ot = jnp.dot(vt, et, preferred_element_type=F32) * jnp.concatenate(inv_cols, axis=1)
            o4 = jnp.concatenate([ot[:, g * QBLK:(g + 1) * QBLK].T for g in range(grp)], axis=1)
            o_ref[0, rows, hk * grp * HEAD_DIM:(hk + 1) * grp * HEAD_DIM] = o4.astype(BF16)
        return carry

    lax.fori_loop(0, tq // QBLK, sub_block, 0)


def _attention(qkv3, sink, attn_width, kv_width, tq):
    b, s, _ = qkv3.shape
    n_heads = attn_width // HEAD_DIM
    n_kv = n_heads // Q_PER_KV
    idx = np.arange(1, n_heads + 1, dtype=np.float32)
    slopes = tuple(float(v) for v in np.power(2.0, -8.0 * idx / n_heads).astype(np.float32))
    kblk = attn_width // kv_width
    return pl.pallas_call(
        functools.partial(_attn_kernel, tq=tq, seq=s, slopes=slopes),
        grid=(b, s // tq),
        in_specs=[
            pl.BlockSpec(memory_space=pltpu.SMEM),
            pl.BlockSpec((1, tq, attn_width), lambda bi, i: (bi, i, 0)),
            pl.BlockSpec((1, s, kv_width), lambda bi, i: (bi, 0, kblk)),
            pl.BlockSpec((1, s, kv_width), lambda bi, i: (bi, 0, kblk + 1)),
        ],
        out_specs=pl.BlockSpec((1, tq, attn_width), lambda bi, i: (bi, i, 0)),
        out_shape=jax.ShapeDtypeStruct((b, s, attn_width), BF16),
        scratch_shapes=[
            pltpu.VMEM((3, n_heads, KWIN, QBLK), F32),
            pltpu.VMEM((n_heads, 1, QBLK), F32),
            pltpu.VMEM((kv_width, s), BF16),
        ],
        compiler_params=pltpu.CompilerParams(
            dimension_semantics=("arbitrary", "arbitrary"), vmem_limit_bytes=V7X_VMEM_LIMIT),
        name="window_attn",
    )(sink, qkv3, qkv3, qkv3)


def _outproj_kernel(yp_ref, ya_ref, w_ref, x_ref, g_ref, o_ref, *, pool_width):
    mix = jnp.dot(yp_ref[...], w_ref[:pool_width, :], preferred_element_type=F32)
    mix = mix + jnp.dot(ya_ref[...], w_ref[pool_width:, :], preferred_element_type=F32)
    o_ref[...] = x_ref[...] + _rms(mix, g_ref[...])


def _out_proj(yp, ya, w_out, x2, g, tm):
    n, d = x2.shape
    pw = yp.shape[1]
    aw = ya.shape[1]
    return pl.pallas_call(
        functools.partial(_outproj_kernel, pool_width=pw),
        grid=(n // tm,),
        in_specs=[
            pl.BlockSpec((tm, pw), lambda i: (i, 0)),
            pl.BlockSpec((tm, aw), lambda i: (i, 0)),
            pl.BlockSpec((pw + aw, d), lambda i: (0, 0)),
            pl.BlockSpec((tm, d), lambda i: (i, 0)),
            pl.BlockSpec((1, d), lambda i: (0, 0)),
        ],
        out_specs=pl.BlockSpec((tm, d), lambda i: (i, 0)),
        out_shape=jax.ShapeDtypeStruct((n, d), F32),
        compiler_params=pltpu.CompilerParams(
            dimension_semantics=("parallel",), vmem_limit_bytes=V7X_VMEM_LIMIT),
        name="out_proj",
    )(yp, ya, w_out, x2, g)


def _gelu_tanh(x):
    return 0.5 * x * (1.0 + jnp.tanh(np.sqrt(2.0 / np.pi).astype(np.float32) * (x + 0.044715 * (x * x * x))))


def _ffn_kernel(x_ref, xp_ref, xn_ref, gpre_ref, wg_ref, wu_ref, cw_ref, cb_ref, wd_ref, gpost_ref,
                o_ref, f_ref, g_ref, *, tm, tiles_per_seq):
    i = pl.program_id(0)
    c = pl.program_id(1)
    hal = CONV_HALO

    @pl.when(c == 0)
    def _():
        it = i % tiles_per_seq
        gp = gpre_ref[...]
        f_ref[hal:hal + tm, :] = _rms(x_ref[...], gp).astype(BF16)
        fp = _rms(xp_ref[...], gp)
        fn = _rms(xn_ref[...], gp)
        row = lax.broadcasted_iota(jnp.int32, (hal, 1), 0)
        fp16 = jnp.concatenate([jnp.zeros_like(fp), fp], axis=0)
        fn16 = jnp.concatenate([fn, jnp.zeros_like(fn)], axis=0)
        prev_row = jnp.where(it > 0, hal - 1, -1)
        next_row = jnp.where(it < tiles_per_seq - 1, 0, -1)
        f_ref[0:hal, :] = jnp.where(row == prev_row, fp16, 0.0).astype(BF16)
        f_ref[hal + tm:, :] = jnp.where(row == next_row, fn16, 0.0).astype(BF16)
        o_ref[...] = jnp.zeros_like(o_ref)

    g_ref[...] = jnp.dot(f_ref[...], wg_ref[0], preferred_element_type=F32)
    up = jnp.dot(f_ref[hal:hal + tm, :], wu_ref[0], preferred_element_type=F32)
    cw = cw_ref[...]
    gate = (g_ref[hal - 1:hal - 1 + tm, :] * cw[0:1, :] + g_ref[hal:hal + tm, :] * cw[1:2, :]
            + g_ref[hal + 1:hal + 1 + tm, :] * cw[2:3, :] + cb_ref[...])
    hmid = (_gelu_tanh(gate) * up).astype(BF16)
    o_ref[...] += jnp.dot(hmid, wd_ref[...], preferred_element_type=F32)

    @pl.when(c == pl.num_programs(1) - 1)
    def _():
        o_ref[...] = x_ref[...] + _rms(o_ref[...], gpost_ref[...])


def _ffn(x1, g_pre, w_gate, w_up, conv_w, conv_b, w_down, g_post, seq, tm):
    n, d = x1.shape
    n_chunks, _, tc = w_gate.shape
    hal = CONV_HALO
    nb8 = tm // 8
    tiles_per_seq = seq // tm
    return pl.pallas_call(
        functools.partial(_ffn_kernel, tm=tm, tiles_per_seq=tiles_per_seq),
        grid=(n // tm, n_chunks),
        in_specs=[
            pl.BlockSpec((tm, d), lambda i, c: (i, 0)),
            pl.BlockSpec((8, d), lambda i, c: (jnp.maximum(i * nb8 - 1, 0), 0)),
            pl.BlockSpec((8, d), lambda i, c: (jnp.minimum((i + 1) * nb8, n // 8 - 1), 0)),
            pl.BlockSpec((1, d), lambda i, c: (0, 0)),
            pl.BlockSpec((1, d, tc), lambda i, c: (c, 0, 0)),
            pl.BlockSpec((1, d, tc), lambda i, c: (c, 0, 0)),
            pl.BlockSpec((3, tc), lambda i, c: (0, c)),
            pl.BlockSpec((1, tc), lambda i, c: (0, c)),
            pl.BlockSpec((tc, d), lambda i, c: (c, 0)),
            pl.BlockSpec((1, d), lambda i, c: (0, 0)),
        ],
        out_specs=pl.BlockSpec((tm, d), lambda i, c: (i, 0)),
        out_shape=jax.ShapeDtypeStruct((n, d), F32),
        scratch_shapes=[
            pltpu.VMEM((tm + 2 * hal, d), BF16),
            pltpu.VMEM((tm + 2 * hal, tc), F32),
        ],
        compiler_params=pltpu.CompilerParams(
            dimension_semantics=("parallel", "arbitrary"), vmem_limit_bytes=V7X_VMEM_LIMIT),
        name="conv_ffn",
    )(x1, x1, x1, g_pre, w_gate, w_up, conv_w, conv_b, w_down, g_post)


def _chunk_cols(w, tc):
    d, n = w.shape
    return w.reshape(d, n // tc, tc).transpose(1, 0, 2).astype(BF16)


def kernel(x, norm_pre_mix, w_in, w_pool, pool_scale, attn_sink, w_out, norm_post_mix, norm_pre_ffn,
           w_gate, w_up, conv_w, conv_b, w_down, norm_post_ffn):
    b, s, d = x.shape
    depth = w_in.shape[0]
    pool_width = pool_scale.shape[1]
    n_heads = attn_sink.shape[1]
    attn_width = n_heads * HEAD_DIM
    kv_width = (w_in.shape[2] - pool_width - attn_width) // 2
    x2 = x.reshape(b * s, d)
    for l in range(depth):
        u, qkv = _in_proj(x2, norm_pre_mix[l][None, :], w_in[l].astype(BF16), pool_width, attn_width, tm=512)
        y_pool = _pool(u.reshape(b, s, pool_width), w_pool[l].astype(BF16), pool_scale[l][None, :], tm=512)
        y_attn = _attention(qkv.reshape(b, s, -1), attn_sink[l], attn_width, kv_width, tq=512)
        x2 = _out_proj(y_pool.reshape(b * s, pool_width), y_attn.reshape(b * s, attn_width),
                       w_out[l].astype(BF16), x2, norm_post_mix[l][None, :], tm=512)
        x2 = _ffn(x2, norm_pre_ffn[l][None, :], _chunk_cols(w_gate[l], FFN_CHUNK), _chunk_cols(w_up[l], FFN_CHUNK),
                  conv_w[l], conv_b[l][None, :], w_down[l].astype(BF16), norm_post_ffn[l][None, :], seq=s, tm=1024)
    return x2.reshape(b, s, d)
```

```python
import functools

import jax
import jax.numpy as jnp
import numpy as np
from jax import lax
from jax.experimental import pallas as pl
from jax.experimental.pallas import tpu as pltpu

F32 = jnp.float32
BF16 = jnp.bfloat16

EPS = 1e-6
LOG2E = 1.4426950408889634
POOL_WINDOWS = (2, 4, 8, 16)
HEAD_DIM = 64
Q_PER_KV = 4
WINDOW = 128
QBLK = 128
KWIN = 3 * QBLK
SUM_ROWS = 16
POOL_HALO = 32
CONV_HALO = 16
FFN_CHUNK = 512
V7X_VMEM_LIMIT = 62 * 1024 * 1024


def _rms(x, g):
    return x * lax.rsqrt(jnp.mean(x * x, axis=-1, keepdims=True) + EPS) * g


def _inproj_kernel(x_ref, g_ref, w_ref, u_ref, qkv_ref, *, pool_width, attn_width):
    h = _rms(x_ref[...], g_ref[...]).astype(BF16)
    u_ref[...] = jnp.dot(h, w_ref[:, :pool_width], preferred_element_type=F32)
    q = jnp.dot(h, w_ref[:, pool_width:pool_width + attn_width], preferred_element_type=F32)
    qkv_ref[:, :attn_width] = (q * (HEAD_DIM ** -0.5 * LOG2E)).astype(BF16)
    kv = jnp.dot(h, w_ref[:, pool_width + attn_width:], preferred_element_type=F32)
    qkv_ref[:, attn_width:] = kv.astype(BF16)


def _in_proj(x2, g, w_in, pool_width, attn_width, tm):
    n, d = x2.shape
    in_width = w_in.shape[1]
    return pl.pallas_call(
        functools.partial(_inproj_kernel, pool_width=pool_width, attn_width=attn_width),
        grid=(n // tm,),
        in_specs=[
            pl.BlockSpec((tm, d), lambda i: (i, 0)),
            pl.BlockSpec((1, d), lambda i: (0, 0)),
            pl.BlockSpec((d, in_width), lambda i: (0, 0)),
        ],
        out_specs=[
            pl.BlockSpec((tm, pool_width), lambda i: (i, 0)),
            pl.BlockSpec((tm, in_width - pool_width), lambda i: (i, 0)),
        ],
        out_shape=[
            jax.ShapeDtypeStruct((n, pool_width), F32),
            jax.ShapeDtypeStruct((n, in_width - pool_width), BF16),
        ],
        compiler_params=pltpu.CompilerParams(
            dimension_semantics=("parallel",), vmem_limit_bytes=V7X_VMEM_LIMIT),
        name="in_proj",
    )(x2, g, w_in)


def _pool_kernel(u_ref, up_ref, un_ref, wp_ref, ps_ref, o_ref, b_ref, s1_ref, s2_ref, s4_ref,
                 *, tm, seq, gw):
    i = pl.program_id(1)
    h = POOL_HALO
    b_ref[0:h, :] = jnp.where(i > 0, up_ref[0], 0.0)
    b_ref[h:h + tm, :] = u_ref[0]
    b_ref[h + tm:, :] = jnp.where(i < pl.num_programs(1) - 1, un_ref[0], 0.0)

    n1 = tm + 2 * h - 16
    a1 = b_ref[8:8 + n1, :] + b_ref[7:7 + n1, :]
    s1_ref[8:8 + n1, :] = a1[:, gw:]
    n2 = n1 - 16
    a2 = s1_ref[17:17 + n2, :] + s1_ref[15:15 + n2, :]
    s2_ref[16:16 + n2, :] = a2[:, gw:]
    n4 = n2 - 16
    a4 = s2_ref[26:26 + n4, :] + s2_ref[22:22 + n4, :]
    s4_ref[24:24 + n4, :] = a4[:, gw:]
    a8 = s4_ref[36:36 + tm, :] + s4_ref[28:28 + tm, :]

    wins = (a1[h - 8:h - 8 + tm, :gw], a2[h - 16:h - 16 + tm, :gw], a4[h - 24:h - 24 + tm, :gw], a8)
    t = i * tm + lax.broadcasted_iota(jnp.int32, (tm, 1), 0)
    for g, w in enumerate(POOL_WINDOWS):
        lo = jnp.maximum(t - w // 2, 0)
        hi = jnp.minimum(t + w // 2, seq)
        inv_cnt = 1.0 / (hi - lo).astype(F32)
        cols = slice(g * gw, (g + 1) * gw)
        d = (wins[g] * inv_cnt - b_ref[h:h + tm, cols]).astype(BF16)
        y = jnp.dot(d, wp_ref[g], preferred_element_type=F32) * ps_ref[:, cols]
        o_ref[0, :, cols] = y.astype(BF16)


def _pool(u3, w_pool, pool_scale, tm):
    b, s, pw = u3.shape
    ng = len(POOL_WINDOWS)
    gw = pw // ng
    h = POOL_HALO
    nh = tm // h
    return pl.pallas_call(
        functools.partial(_pool_kernel, tm=tm, seq=s, gw=gw),
        grid=(b, s // tm),
        in_specs=[
            pl.BlockSpec((1, tm, pw), lambda bi, i: (bi, i, 0)),
            pl.BlockSpec((1, h, pw), lambda bi, i: (bi, jnp.maximum(i * nh - 1, 0), 0)),
            pl.BlockSpec((1, h, pw), lambda bi, i: (bi, jnp.minimum((i + 1) * nh, s // h - 1), 0)),
            pl.BlockSpec((ng, gw, gw), lambda bi, i: (0, 0, 0)),
            pl.BlockSpec((1, pw), lambda bi, i: (0, 0)),
        ],
        out_specs=pl.BlockSpec((1, tm, pw), lambda bi, i: (bi, i, 0)),
        out_shape=jax.ShapeDtypeStruct((b, s, pw), BF16),
        scratch_shapes=[
            pltpu.VMEM((tm + 2 * h, pw), F32),
            pltpu.VMEM((tm + 2 * h, pw - gw), F32),
            pltpu.VMEM((tm + 2 * h, pw - 2 * gw), F32),
            pltpu.VMEM((tm + 2 * h, pw - 3 * gw), F32),
        ],
        compiler_params=pltpu.CompilerParams(
            dimension_semantics=("parallel", "parallel"), vmem_limit_bytes=V7X_VMEM_LIMIT),
        name="pool_mixer",
    )(u3, u3, u3, w_pool, pool_scale)


def _attn_kernel(sink_ref, q_ref, k_ref, v_ref, o_ref, bias_ref, sinkrow_ref, vt_ref, st_ref, e_ref,
                 *, tq, seq, slopes):
    bi = pl.program_id(0)
    i = pl.program_id(1)
    grp = Q_PER_KV
    n_heads = len(slopes)
    n_kv = n_heads // grp

    @pl.when((bi == 0) & (i == 0))
    def _():
        kj = lax.broadcasted_iota(jnp.int32, (KWIN, QBLK), 0)
        qi = lax.broadcasted_iota(jnp.int32, (KWIN, QBLK), 1)
        for v in range(3):
            dist_i = jnp.abs(v * QBLK + qi - kj)
            valid = dist_i <= WINDOW
            dist = dist_i.astype(F32)
            for h in range(n_heads):
                cols = slice((h % grp) * QBLK, (h % grp + 1) * QBLK)
                bias_ref[v, h // grp, :, cols] = jnp.where(valid, -(slopes[h] * dist) * LOG2E, -jnp.inf)
        for h in range(n_heads):
            cols = slice((h % grp) * QBLK, (h % grp + 1) * QBLK)
            sinkrow_ref[h // grp, :, cols] = jnp.full((1, QBLK), sink_ref[h] * LOG2E, F32)

    @pl.when(i == 0)
    def _():
        def xpose(cb, carry):
            blk = pl.ds(pl.multiple_of(cb * QBLK, QBLK), QBLK)
            vt_ref[:, blk] = v_ref[0, blk, :].T
            return carry
        lax.fori_loop(0, seq // QBLK, xpose, 0)

    def sub_block(j, carry):
        q0 = i * tq + j * QBLK
        k0 = pl.multiple_of(jnp.clip(q0 - WINDOW, 0, seq - KWIN), QBLK)
        variant = (q0 - k0) // QBLK
        rows = pl.ds(pl.multiple_of(j * QBLK, QBLK), QBLK)
        par = (i + j) % 2
        m_all = []
        for hk in range(n_kv):
            q4 = q_ref[0, rows, hk * grp * HEAD_DIM:(hk + 1) * grp * HEAD_DIM]
            q = jnp.concatenate([q4[:, g * HEAD_DIM:(g + 1) * HEAD_DIM] for g in range(grp)], axis=0)
            k = k_ref[0, pl.ds(k0, KWIN), hk * HEAD_DIM:(hk + 1) * HEAD_DIM]
            st = lax.dot_general(k, q, (((1,), (1,)), ((), ())), preferred_element_type=F32)
            st_ref[par, hk] = st + bias_ref[variant, hk]
            m_all.append(jnp.maximum(jnp.max(st_ref[par, hk], axis=0, keepdims=True), sinkrow_ref[hk]))
        for hk in range(n_kv):
            e_ref[par, hk] = jnp.exp2(st_ref[par, hk] - m_all[hk]).astype(BF16)
        ones = jnp.ones((SUM_ROWS, KWIN), BF16)
        for hk in range(n_kv):
            vt = jnp.concatenate([vt_ref[hk * HEAD_DIM:(hk + 1) * HEAD_DIM, pl.ds(k0, KWIN)], ones], axis=0)
            ot = jnp.dot(vt, e_ref[par, hk], preferred_element_type=F32)
            den = ot[HEAD_DIM:HEAD_DIM + 1, :] + jnp.exp2(sinkrow_ref[hk] - m_all[hk])
            ot = ot[:HEAD_DIM, :] * (1.0 / den)
            o4 = jnp.concatenate([ot[:, g * QBLK:(g + 1) * QBLK].T for g in range(grp)], axis=1)
            o_ref[0, rows, hk * grp * HEAD_DIM:(hk + 1) * grp * HEAD_DIM] = o4.astype(BF16)
        return carry

    for j in range(tq // QBLK):
        sub_block(j, 0)


def _attention(qkv3, sink, attn_width, kv_width, tq):
    b, s, _ = qkv3.shape
    n_heads = attn_width // HEAD_DIM
    n_kv = n_heads // Q_PER_KV
    idx = np.arange(1, n_heads + 1, dtype=np.float32)
    slopes = tuple(float(v) for v in np.power(2.0, -8.0 * idx / n_heads).astype(np.float32))
    kblk = attn_width // kv_width
    return pl.pallas_call(
        functools.partial(_attn_kernel, tq=tq, seq=s, slopes=slopes),
        grid=(b, s // tq),
        in_specs=[
            pl.BlockSpec(memory_space=pltpu.SMEM),
            pl.BlockSpec((1, tq, attn_width), lambda bi, i: (bi, i, 0)),
            pl.BlockSpec((1, s, kv_width), lambda bi, i: (bi, 0, kblk)),
            pl.BlockSpec((1, s, kv_width), lambda bi, i: (bi, 0, kblk + 1)),
        ],
        out_specs=pl.BlockSpec((1, tq, attn_width), lambda bi, i: (bi, i, 0)),
        out_shape=jax.ShapeDtypeStruct((b, s, attn_width), BF16),
        scratch_shapes=[
            pltpu.VMEM((3, n_kv, KWIN, Q_PER_KV * QBLK), F32),
            pltpu.VMEM((n_kv, 1, Q_PER_KV * QBLK), F32),
            pltpu.VMEM((kv_width, s), BF16),
            pltpu.VMEM((2, n_kv, KWIN, Q_PER_KV * QBLK), F32),
            pltpu.VMEM((2, n_kv, KWIN, Q_PER_KV * QBLK), BF16),
        ],
        compiler_params=pltpu.CompilerParams(
            dimension_semantics=("arbitrary", "arbitrary"), vmem_limit_bytes=V7X_VMEM_LIMIT),
        name="window_attn",
    )(sink, qkv3, qkv3, qkv3)


def _outproj_kernel(yp_ref, ya_ref, w_ref, x_ref, g_ref, o_ref, *, pool_width):
    mix = jnp.dot(yp_ref[...], w_ref[:pool_width, :], preferred_element_type=F32)
    mix = mix + jnp.dot(ya_ref[...], w_ref[pool_width:, :], preferred_element_type=F32)
    o_ref[...] = x_ref[...] + _rms(mix, g_ref[...])


def _out_proj(yp, ya, w_out, x2, g, tm):
    n, d = x2.shape
    pw = yp.shape[1]
    aw = ya.shape[1]
    return pl.pallas_call(
        functools.partial(_outproj_kernel, pool_width=pw),
        grid=(n // tm,),
        in_specs=[
            pl.BlockSpec((tm, pw), lambda i: (i, 0)),
            pl.BlockSpec((tm, aw), lambda i: (i, 0)),
            pl.BlockSpec((pw + aw, d), lambda i: (0, 0)),
            pl.BlockSpec((tm, d), lambda i: (i, 0)),
            pl.BlockSpec((1, d), lambda i: (0, 0)),
        ],
        out_specs=pl.BlockSpec((tm, d), lambda i: (i, 0)),
        out_shape=jax.ShapeDtypeStruct((n, d), F32),
        compiler_params=pltpu.CompilerParams(
            dimension_semantics=("parallel",), vmem_limit_bytes=V7X_VMEM_LIMIT),
        name="out_proj",
    )(yp, ya, w_out, x2, g)


def _gelu_tanh(x):
    return 0.5 * x * (1.0 + jnp.tanh(np.sqrt(2.0 / np.pi).astype(np.float32) * (x + 0.044715 * (x * x * x))))


def _ffn_kernel(x_ref, xp_ref, xn_ref, gpre_ref, wg_ref, wu_ref, cw_ref, cb_ref, wd_ref, gpost_ref,
                o_ref, f_ref, g_ref, *, tm, tiles_per_seq):
    i = pl.program_id(0)
    c = pl.program_id(1)
    hal = CONV_HALO

    @pl.when(c == 0)
    def _():
        it = i % tiles_per_seq
        gp = gpre_ref[...]
        f_ref[hal:hal + tm, :] = _rms(x_ref[...], gp).astype(BF16)
        fp = _rms(xp_ref[...], gp)
        fn = _rms(xn_ref[...], gp)
        row = lax.broadcasted_iota(jnp.int32, (hal, 1), 0)
        fp16 = jnp.concatenate([jnp.zeros_like(fp), fp], axis=0)
        fn16 = jnp.concatenate([fn, jnp.zeros_like(fn)], axis=0)
        prev_row = jnp.where(it > 0, hal - 1, -1)
        next_row = jnp.where(it < tiles_per_seq - 1, 0, -1)
        f_ref[0:hal, :] = jnp.where(row == prev_row, fp16, 0.0).astype(BF16)
        f_ref[hal + tm:, :] = jnp.where(row == next_row, fn16, 0.0).astype(BF16)
        o_ref[...] = jnp.zeros_like(o_ref)

    g_ref[...] = jnp.dot(f_ref[...], wg_ref[...], preferred_element_type=F32)
    up = jnp.dot(f_ref[hal:hal + tm, :], wu_ref[...], preferred_element_type=F32)
    cw = cw_ref[...]
    gate = (g_ref[hal - 1:hal - 1 + tm, :] * cw[0:1, :] + g_ref[hal:hal + tm, :] * cw[1:2, :]
            + g_ref[hal + 1:hal + 1 + tm, :] * cw[2:3, :] + cb_ref[...])
    hmid = (_gelu_tanh(gate) * up).astype(BF16)
    o_ref[...] += jnp.dot(hmid, wd_ref[...], preferred_element_type=F32)

    @pl.when(c == pl.num_programs(1) - 1)
    def _():
        o_ref[...] = x_ref[...] + _rms(o_ref[...], gpost_ref[...])


def _ffn(x1, g_pre, w_gate, w_up, conv_w, conv_b, w_down, g_post, seq, tm, tc):
    n, d = x1.shape
    n_chunks = w_gate.shape[1] // tc
    hal = CONV_HALO
    nb8 = tm // 8
    tiles_per_seq = seq // tm
    return pl.pallas_call(
        functools.partial(_ffn_kernel, tm=tm, tiles_per_seq=tiles_per_seq),
        grid=(n // tm, n_chunks),
        in_specs=[
            pl.BlockSpec((tm, d), lambda i, c: (i, 0)),
            pl.BlockSpec((8, d), lambda i, c: (jnp.maximum(i * nb8 - 1, 0), 0)),
            pl.BlockSpec((8, d), lambda i, c: (jnp.minimum((i + 1) * nb8, n // 8 - 1), 0)),
            pl.BlockSpec((1, d), lambda i, c: (0, 0)),
            pl.BlockSpec((d, tc), lambda i, c: (0, c)),
            pl.BlockSpec((d, tc), lambda i, c: (0, c)),
            pl.BlockSpec((3, tc), lambda i, c: (0, c)),
            pl.BlockSpec((1, tc), lambda i, c: (0, c)),
            pl.BlockSpec((tc, d), lambda i, c: (c, 0)),
            pl.BlockSpec((1, d), lambda i, c: (0, 0)),
        ],
        out_specs=pl.BlockSpec((tm, d), lambda i, c: (i, 0)),
        out_shape=jax.ShapeDtypeStruct((n, d), F32),
        scratch_shapes=[
            pltpu.VMEM((tm + 2 * hal, d), BF16),
            pltpu.VMEM((tm + 2 * hal, tc), F32),
        ],
        compiler_params=pltpu.CompilerParams(
            dimension_semantics=("parallel", "arbitrary"), vmem_limit_bytes=V7X_VMEM_LIMIT),
        name="conv_ffn",
    )(x1, x1, x1, g_pre, w_gate, w_up, conv_w, conv_b, w_down, g_post)


def kernel(x, norm_pre_mix, w_in, w_pool, pool_scale, attn_sink, w_out, norm_post_mix, norm_pre_ffn,
           w_gate, w_up, conv_w, conv_b, w_down, norm_post_ffn):
    b, s, d = x.shape
    depth = w_in.shape[0]
    pool_width = pool_scale.shape[1]
    n_heads = attn_sink.shape[1]
    attn_width = n_heads * HEAD_DIM
    kv_width = (w_in.shape[2] - pool_width - attn_width) // 2
    x2 = x.reshape(b * s, d)
    for l in range(depth):
        u, qkv = _in_proj(x2, norm_pre_mix[l][None, :], w_in[l].astype(BF16), pool_width, attn_width, tm=512)
        y_pool = _pool(u.reshape(b, s, pool_width), w_pool[l].astype(BF16), pool_scale[l][None, :], tm=512)
        y_attn = _attention(qkv.reshape(b, s, -1), attn_sink[l], attn_width, kv_width, tq=512)
        x2 = _out_proj(y_pool.reshape(b * s, pool_width), y_attn.reshape(b * s, attn_width),
                       w_out[l].astype(BF16), x2, norm_post_mix[l][None, :], tm=512)
        x2 = _ffn(x2, norm_pre_ffn[l][None, :], w_gate[l].astype(BF16), w_up[l].astype(BF16), conv_w[l],
                  conv_b[l][None, :], w_down[l].astype(BF16), norm_post_ffn[l][None, :], seq=s, tm=1024,
                  tc=FFN_CHUNK)
    return x2.reshape(b, s, d)
```

```python
import functools

import jax
import jax.numpy as jnp
import numpy as np
from jax import lax
from jax.experimental import pallas as pl
from jax.experimental.pallas import tpu as pltpu

F32 = jnp.float32
BF16 = jnp.bfloat16

EPS = 1e-6
LOG2E = 1.4426950408889634
POOL_WINDOWS = (2, 4, 8, 16)
HEAD_DIM = 64
Q_PER_KV = 4
WINDOW = 128
QBLK = 128
KWIN = 3 * QBLK
SUM_ROWS = 16
IN_HALO = 16
CONV_HALO = 16
FFN_CHUNK = 512
V7X_VMEM_LIMIT = 62 * 1024 * 1024


def _rms(x, g):
    return x * lax.rsqrt(jnp.mean(x * x, axis=-1, keepdims=True) + EPS) * g


def _roll_rows(a, shift):
    return pltpu.roll(a, shift % a.shape[0], axis=0)


def _inproj_pool_kernel(x_ref, xp_ref, xn_ref, g_ref, w_ref, wp_ref, ps_ref, yp_ref, qkv_ref, h_ref,
                        *, tm, seq, pool_width, attn_width):
    i = pl.program_id(0)
    tiles_per_seq = seq // tm
    it = i % tiles_per_seq
    hal = IN_HALO
    half = tm // 2
    rows = tm + 2 * hal
    gw = pool_width // len(POOL_WINDOWS)
    gp = g_ref[...]

    h_ref[0:hal, :] = jnp.where(it > 0, _rms(xp_ref[...], gp), 0.0).astype(BF16)
    h_ref[hal:hal + half, :] = _rms(x_ref[0:half, :], gp).astype(BF16)
    h_ref[hal + half:hal + tm, :] = _rms(x_ref[half:tm, :], gp).astype(BF16)
    h_ref[hal + tm:, :] = jnp.where(it < tiles_per_seq - 1, _rms(xn_ref[...], gp), 0.0).astype(BF16)

    w_u = w_ref[:, :pool_width]
    u = jnp.concatenate([jnp.dot(h_ref[0:hal + half, :], w_u, preferred_element_type=F32),
                         jnp.dot(h_ref[hal + half:rows, :], w_u, preferred_element_type=F32)], axis=0)
    w_q = w_ref[:, pool_width:pool_width + attn_width]
    w_kv = w_ref[:, pool_width + attn_width:]
    for r0 in (0, half):
        hm = h_ref[hal + r0:hal + r0 + half, :]
        q = jnp.dot(hm, w_q, preferred_element_type=F32)
        qkv_ref[r0:r0 + half, :attn_width] = (q * (HEAD_DIM ** -0.5 * LOG2E)).astype(BF16)
        qkv_ref[r0:r0 + half, attn_width:] = jnp.dot(hm, w_kv, preferred_element_type=F32).astype(BF16)

    a1 = u + _roll_rows(u, 1)
    a1r = a1[:, gw:]
    a2 = _roll_rows(a1r, -1) + _roll_rows(a1r, 1)
    a2r = a2[:, gw:]
    a4 = _roll_rows(a2r, -2) + _roll_rows(a2r, 2)
    a4r = a4[:, gw:]
    a8 = _roll_rows(a4r, -4) + _roll_rows(a4r, 4)
    main = slice(hal, hal + tm)
    wins = (a1[main, :gw], a2[main, :gw], a4[main, :gw], a8[main, :])
    t = it * tm + lax.broadcasted_iota(jnp.int32, (tm, 1), 0)
    for g, w in enumerate(POOL_WINDOWS):
        lo = jnp.maximum(t - w // 2, 0)
        hi = jnp.minimum(t + w // 2, seq)
        inv_cnt = 1.0 / (hi - lo).astype(F32)
        cols = slice(g * gw, (g + 1) * gw)
        d = (wins[g] * inv_cnt - u[main, cols]).astype(BF16)
        y = jnp.dot(d, wp_ref[g], preferred_element_type=F32) * ps_ref[:, cols]
        yp_ref[:, cols] = y.astype(BF16)


def _in_proj_pool(x2, g, w_in, w_pool, pool_scale, seq, pool_width, attn_width, tm):
    n, d = x2.shape
    in_width = w_in.shape[1]
    ng = len(POOL_WINDOWS)
    gw = pool_width // ng
    hal = IN_HALO
    nh = tm // hal
    resident = dict(pipeline_mode=pl.Buffered(1))
    return pl.pallas_call(
        functools.partial(_inproj_pool_kernel, tm=tm, seq=seq, pool_width=pool_width, attn_width=attn_width),
        grid=(n // tm,),
        in_specs=[
            pl.BlockSpec((tm, d), lambda i: (i, 0)),
            pl.BlockSpec((hal, d), lambda i: (jnp.maximum(i * nh - 1, 0), 0)),
            pl.BlockSpec((hal, d), lambda i: (jnp.minimum((i + 1) * nh, n // hal - 1), 0)),
            pl.BlockSpec((1, d), lambda i: (0, 0)),
            pl.BlockSpec((d, in_width), lambda i: (0, 0), **resident),
            pl.BlockSpec((ng, gw, gw), lambda i: (0, 0, 0), **resident),
            pl.BlockSpec((1, pool_width), lambda i: (0, 0)),
        ],
        out_specs=[
            pl.BlockSpec((tm, pool_width), lambda i: (i, 0)),
            pl.BlockSpec((tm, in_width - pool_width), lambda i: (i, 0)),
        ],
        out_shape=[
            jax.ShapeDtypeStruct((n, pool_width), BF16),
            jax.ShapeDtypeStruct((n, in_width - pool_width), BF16),
        ],
        scratch_shapes=[pltpu.VMEM((tm + 2 * hal, d), BF16)],
        compiler_params=pltpu.CompilerParams(
            dimension_semantics=("parallel",), vmem_limit_bytes=V7X_VMEM_LIMIT),
        name="in_proj_pool",
    )(x2, x2, x2, g, w_in, w_pool, pool_scale)


def _attn_kernel(sink_ref, q_ref, k_ref, v_ref, o_ref, bias_ref, sinkrow_ref, vt_ref, st_ref, e_ref,
                 *, tq, seq, slopes):
    bi = pl.program_id(0)
    i = pl.program_id(1)
    grp = Q_PER_KV
    n_heads = len(slopes)
    n_kv = n_heads // grp

    @pl.when((bi == 0) & (i == 0))
    def _():
        kj = lax.broadcasted_iota(jnp.int32, (KWIN, QBLK), 0)
        qi = lax.broadcasted_iota(jnp.int32, (KWIN, QBLK), 1)
        for v in range(3):
            dist_i = jnp.abs(v * QBLK + qi - kj)
            valid = dist_i <= WINDOW
            dist = dist_i.astype(F32)
            for h in range(n_heads):
                cols = slice((h % grp) * QBLK, (h % grp + 1) * QBLK)
                bias_ref[v, h // grp, :, cols] = jnp.where(valid, -(slopes[h] * dist) * LOG2E, -jnp.inf)
        for h in range(n_heads):
            cols = slice((h % grp) * QBLK, (h % grp + 1) * QBLK)
            sinkrow_ref[h // grp, :, cols] = jnp.full((1, QBLK), sink_ref[h] * LOG2E, F32)

    @pl.when(i == 0)
    def _():
        def xpose(cb, carry):
            blk = pl.ds(pl.multiple_of(cb * QBLK, QBLK), QBLK)
            vt_ref[:, blk] = v_ref[0, blk, :].T
            return carry
        lax.fori_loop(0, seq // QBLK, xpose, 0)

    def sub_block(j, carry):
        q0 = i * tq + j * QBLK
        k0 = pl.multiple_of(jnp.clip(q0 - WINDOW, 0, seq - KWIN), QBLK)
        variant = (q0 - k0) // QBLK
        rows = pl.ds(pl.multiple_of(j * QBLK, QBLK), QBLK)
        par = (i + j) % 2
        m_all = []
        for hk in range(n_kv):
            q4 = q_ref[0, rows, hk * grp * HEAD_DIM:(hk + 1) * grp * HEAD_DIM]
            q = jnp.concatenate([q4[:, g * HEAD_DIM:(g + 1) * HEAD_DIM] for g in range(grp)], axis=0)
            k = k_ref[0, pl.ds(k0, KWIN), hk * HEAD_DIM:(hk + 1) * HEAD_DIM]
            st = lax.dot_general(k, q, (((1,), (1,)), ((), ())), preferred_element_type=F32)
            st_ref[par, hk] = st + bias_ref[variant, hk]
            m_all.append(jnp.maximum(jnp.max(st_ref[par, hk], axis=0, keepdims=True), sinkrow_ref[hk]))
        for hk in range(n_kv):
            e_ref[par, hk] = jnp.exp2(st_ref[par, hk] - m_all[hk]).astype(BF16)
        ones = jnp.ones((SUM_ROWS, KWIN), BF16)
        for hk in range(n_kv):
            vt = jnp.concatenate([vt_ref[hk * HEAD_DIM:(hk + 1) * HEAD_DIM, pl.ds(k0, KWIN)], ones], axis=0)
            ot = jnp.dot(vt, e_ref[par, hk], preferred_element_type=F32)
            den = ot[HEAD_DIM:HEAD_DIM + 1, :] + jnp.exp2(sinkrow_ref[hk] - m_all[hk])
            ot = ot[:HEAD_DIM, :] * (1.0 / den)
            o4 = jnp.concatenate([ot[:, g * QBLK:(g + 1) * QBLK].T for g in range(grp)], axis=1)
            o_ref[0, rows, hk * grp * HEAD_DIM:(hk + 1) * grp * HEAD_DIM] = o4.astype(BF16)
        return carry

    for j in range(tq // QBLK):
        sub_block(j, 0)


def _attention(qkv3, sink, attn_width, kv_width, tq):
    b, s, _ = qkv3.shape
    n_heads = attn_width // HEAD_DIM
    n_kv = n_heads // Q_PER_KV
    idx = np.arange(1, n_heads + 1, dtype=np.float32)
    slopes = tuple(float(v) for v in np.power(2.0, -8.0 * idx / n_heads).astype(np.float32))
    kblk = attn_width // kv_width
    return pl.pallas_call(
        functools.partial(_attn_kernel, tq=tq, seq=s, slopes=slopes),
        grid=(b, s // tq),
        in_specs=[
            pl.BlockSpec(memory_space=pltpu.SMEM),
            pl.BlockSpec((1, tq, attn_width), lambda bi, i: (bi, i, 0)),
            pl.BlockSpec((1, s, kv_width), lambda bi, i: (bi, 0, kblk)),
            pl.BlockSpec((1, s, kv_width), lambda bi, i: (bi, 0, kblk + 1)),
        ],
        out_specs=pl.BlockSpec((1, tq, attn_width), lambda bi, i: (bi, i, 0)),
        out_shape=jax.ShapeDtypeStruct((b, s, attn_width), BF16),
        scratch_shapes=[
            pltpu.VMEM((3, n_kv, KWIN, Q_PER_KV * QBLK), F32),
            pltpu.VMEM((n_kv, 1, Q_PER_KV * QBLK), F32),
            pltpu.VMEM((kv_width, s), BF16),
            pltpu.VMEM((2, n_kv, KWIN, Q_PER_KV * QBLK), F32),
            pltpu.VMEM((2, n_kv, KWIN, Q_PER_KV * QBLK), BF16),
        ],
        compiler_params=pltpu.CompilerParams(
            dimension_semantics=("arbitrary", "arbitrary"), vmem_limit_bytes=V7X_VMEM_LIMIT),
        name="window_attn",
    )(sink, qkv3, qkv3, qkv3)


def _outproj_kernel(yp_ref, ya_ref, w_ref, x_ref, g_ref, o_ref, *, pool_width, tm):
    half = tm // 2
    for r0 in (0, half):
        rows = slice(r0, r0 + half)
        mix = jnp.dot(yp_ref[rows, :], w_ref[:pool_width, :], preferred_element_type=F32)
        mix = mix + jnp.dot(ya_ref[rows, :], w_ref[pool_width:, :], preferred_element_type=F32)
        o_ref[rows, :] = x_ref[rows, :] + _rms(mix, g_ref[...])


def _out_proj(yp, ya, w_out, x2, g, tm):
    n, d = x2.shape
    pw = yp.shape[1]
    aw = ya.shape[1]
    return pl.pallas_call(
        functools.partial(_outproj_kernel, pool_width=pw, tm=tm),
        grid=(n // tm,),
        in_specs=[
            pl.BlockSpec((tm, pw), lambda i: (i, 0)),
            pl.BlockSpec((tm, aw), lambda i: (i, 0)),
            pl.BlockSpec((pw + aw, d), lambda i: (0, 0), pipeline_mode=pl.Buffered(1)),
            pl.BlockSpec((tm, d), lambda i: (i, 0)),
            pl.BlockSpec((1, d), lambda i: (0, 0)),
        ],
        out_specs=pl.BlockSpec((tm, d), lambda i: (i, 0)),
        out_shape=jax.ShapeDtypeStruct((n, d), F32),
        compiler_params=pltpu.CompilerParams(
            dimension_semantics=("parallel",), vmem_limit_bytes=V7X_VMEM_LIMIT),
        name="out_proj",
    )(yp, ya, w_out, x2, g)


def _gelu_tanh(x):
    return 0.5 * x * (1.0 + jnp.tanh(np.sqrt(2.0 / np.pi).astype(np.float32) * (x + 0.044715 * (x * x * x))))


def _ffn_kernel(x_ref, xp_ref, xn_ref, gpre_ref, wg_ref, wu_ref, cw_ref, cb_ref, wd_ref, gpost_ref,
                o_ref, f_ref, *, tm, tiles_per_seq):
    i = pl.program_id(0)
    c = pl.program_id(1)
    hal = CONV_HALO

    @pl.when(c == 0)
    def _():
        it = i % tiles_per_seq
        gp = gpre_ref[...]
        f_ref[hal:hal + tm, :] = _rms(x_ref[...], gp).astype(BF16)
        fp = _rms(xp_ref[...], gp)
        fn = _rms(xn_ref[...], gp)
        row = lax.broadcasted_iota(jnp.int32, (hal, 1), 0)
        fp16 = jnp.concatenate([jnp.zeros_like(fp), fp], axis=0)
        fn16 = jnp.concatenate([fn, jnp.zeros_like(fn)], axis=0)
        prev_row = jnp.where(it > 0, hal - 1, -1)
        next_row = jnp.where(it < tiles_per_seq - 1, 0, -1)
        f_ref[0:hal, :] = jnp.where(row == prev_row, fp16, 0.0).astype(BF16)
        f_ref[hal + tm:, :] = jnp.where(row == next_row, fn16, 0.0).astype(BF16)
        o_ref[...] = jnp.zeros_like(o_ref)

    g = jnp.dot(f_ref[...], wg_ref[...], preferred_element_type=F32)
    up = jnp.dot(f_ref[hal:hal + tm, :], wu_ref[...], preferred_element_type=F32)
    cw = cw_ref[...]
    g_prev = pltpu.roll(g, 1, axis=0)[hal:hal + tm, :]
    g_next = pltpu.roll(g, tm + 2 * hal - 1, axis=0)[hal:hal + tm, :]
    gate = g_prev * cw[0:1, :] + g[hal:hal + tm, :] * cw[1:2, :] + g_next * cw[2:3, :] + cb_ref[...]
    hmid = (_gelu_tanh(gate) * up).astype(BF16)
    o_ref[...] += jnp.dot(hmid, wd_ref[...], preferred_element_type=F32)

    @pl.when(c == pl.num_programs(1) - 1)
    def _():
        o_ref[...] = x_ref[...] + _rms(o_ref[...], gpost_ref[...])


def _ffn(x1, g_pre, w_gate, w_up, conv_w, conv_b, w_down, g_post, seq, tm, tc):
    n, d = x1.shape
    n_chunks = w_gate.shape[1] // tc
    hal = CONV_HALO
    nb8 = tm // 8
    tiles_per_seq = seq // tm
    return pl.pallas_call(
        functools.partial(_ffn_kernel, tm=tm, tiles_per_seq=tiles_per_seq),
        grid=(n // tm, n_chunks),
        in_specs=[
            pl.BlockSpec((tm, d), lambda i, c: (i, 0)),
            pl.BlockSpec((8, d), lambda i, c: (jnp.maximum(i * nb8 - 1, 0), 0)),
            pl.BlockSpec((8, d), lambda i, c: (jnp.minimum((i + 1) * nb8, n // 8 - 1), 0)),
            pl.BlockSpec((1, d), lambda i, c: (0, 0)),
            pl.BlockSpec((d, tc), lambda i, c: (0, c)),
            pl.BlockSpec((d, tc), lambda i, c: (0, c)),
            pl.BlockSpec((3, tc), lambda i, c: (0, c)),
            pl.BlockSpec((1, tc), lambda i, c: (0, c)),
            pl.BlockSpec((tc, d), lambda i, c: (c, 0)),
            pl.BlockSpec((1, d), lambda i, c: (0, 0)),
        ],
        out_specs=pl.BlockSpec((tm, d), lambda i, c: (i, 0)),
        out_shape=jax.ShapeDtypeStruct((n, d), F32),
        scratch_shapes=[
            pltpu.VMEM((tm + 2 * hal, d), BF16),
        ],
        compiler_params=pltpu.CompilerParams(
            dimension_semantics=("parallel", "arbitrary"), vmem_limit_bytes=V7X_VMEM_LIMIT),
        name="conv_ffn",
    )(x1, x1, x1, g_pre, w_gate, w_up, conv_w, conv_b, w_down, g_post)


def kernel(x, norm_pre_mix, w_in, w_pool, pool_scale, attn_sink, w_out, norm_post_mix, norm_pre_ffn,
           w_gate, w_up, conv_w, conv_b, w_down, norm_post_ffn):
    b, s, d = x.shape
    depth = w_in.shape[0]
    pool_width = pool_scale.shape[1]
    n_heads = attn_sink.shape[1]
    attn_width = n_heads * HEAD_DIM
    kv_width = (w_in.shape[2] - pool_width - attn_width) // 2
    x2 = x.reshape(b * s, d)
    for l in range(depth):
        y_pool, qkv = _in_proj_pool(x2, norm_pre_mix[l][None, :], w_in[l].astype(BF16), w_pool[l].astype(BF16),
                                    pool_scale[l][None, :], s, pool_width, attn_width, tm=1024)
        y_attn = _attention(qkv.reshape(b, s, -1), attn_sink[l], attn_width, kv_width, tq=512)
        x2 = _out_proj(y_pool, y_attn.reshape(b * s, attn_width),
                       w_out[l].astype(BF16), x2, norm_post_mix[l][None, :], tm=1024)
        x2 = _ffn(x2, norm_pre_ffn[l][None, :], w_gate[l].astype(BF16), w_up[l].astype(BF16), conv_w[l],
                  conv_b[l][None, :], w_down[l].astype(BF16), norm_post_ffn[l][None, :], seq=s, tm=1024,
                  tc=FFN_CHUNK)
    return x2.reshape(b, s, d)
```

```python
import functools

import jax
import jax.numpy as jnp
import numpy as np
from jax import lax
from jax.experimental import pallas as pl
from jax.experimental.pallas import tpu as pltpu

F32 = jnp.float32
BF16 = jnp.bfloat16

EPS = 1e-6
LOG2E = 1.4426950408889634
POOL_WINDOWS = (2, 4, 8, 16)
HEAD_DIM = 64
Q_PER_KV = 4
WINDOW = 128
QBLK = 128
KWIN = 3 * QBLK
SUM_ROWS = 16
IN_HALO = 16
CONV_HALO = 16
FFN_CHUNK = 512
V7X_VMEM_LIMIT = 62 * 1024 * 1024


def _rms(x, g):
    return x * lax.rsqrt(jnp.mean(x * x, axis=-1, keepdims=True) + EPS) * g


def _roll_rows(a, shift):
    return pltpu.roll(a, shift % a.shape[0], axis=0)


def _inproj_pool_kernel(x_ref, xp_ref, xn_ref, g_ref, w_ref, wp_ref, ps_ref, ca_ref, cb_ref,
                        yp_ref, qkv_ref, ca_out_ref, cb_out_ref, h_ref, *, tm, seq, pool_width, attn_width):
    ca_out_ref[...] = ca_ref[...].astype(BF16)
    cb_out_ref[...] = cb_ref[...].astype(BF16)
    i = pl.program_id(0)
    tiles_per_seq = seq // tm
    it = i % tiles_per_seq
    hal = IN_HALO
    half = tm // 2
    rows = tm + 2 * hal
    gw = pool_width // len(POOL_WINDOWS)
    gp = g_ref[...]

    h_ref[0:hal, :] = jnp.where(it > 0, _rms(xp_ref[...], gp), 0.0).astype(BF16)
    h_ref[hal:hal + half, :] = _rms(x_ref[0:half, :], gp).astype(BF16)
    h_ref[hal + half:hal + tm, :] = _rms(x_ref[half:tm, :], gp).astype(BF16)
    h_ref[hal + tm:, :] = jnp.where(it < tiles_per_seq - 1, _rms(xn_ref[...], gp), 0.0).astype(BF16)

    w_u = w_ref[:, :pool_width]
    u = jnp.concatenate([jnp.dot(h_ref[0:hal + half, :], w_u, preferred_element_type=F32),
                         jnp.dot(h_ref[hal + half:rows, :], w_u, preferred_element_type=F32)], axis=0)
    w_q = w_ref[:, pool_width:pool_width + attn_width]
    w_kv = w_ref[:, pool_width + attn_width:]
    for r0 in (0, half):
        hm = h_ref[hal + r0:hal + r0 + half, :]
        q = jnp.dot(hm, w_q, preferred_element_type=F32)
        qkv_ref[r0:r0 + half, :attn_width] = (q * (HEAD_DIM ** -0.5 * LOG2E)).astype(BF16)
        qkv_ref[r0:r0 + half, attn_width:] = jnp.dot(hm, w_kv, preferred_element_type=F32).astype(BF16)

    a1 = u + _roll_rows(u, 1)
    a1r = a1[:, gw:]
    a2 = _roll_rows(a1r, -1) + _roll_rows(a1r, 1)
    a2r = a2[:, gw:]
    a4 = _roll_rows(a2r, -2) + _roll_rows(a2r, 2)
    a4r = a4[:, gw:]
    a8 = _roll_rows(a4r, -4) + _roll_rows(a4r, 4)
    main = slice(hal, hal + tm)
    wins = (a1[main, :gw], a2[main, :gw], a4[main, :gw], a8[main, :])
    t = it * tm + lax.broadcasted_iota(jnp.int32, (tm, 1), 0)
    for g, w in enumerate(POOL_WINDOWS):
        lo = jnp.maximum(t - w // 2, 0)
        hi = jnp.minimum(t + w // 2, seq)
        inv_cnt = 1.0 / (hi - lo).astype(F32)
        cols = slice(g * gw, (g + 1) * gw)
        d = (wins[g] * inv_cnt - u[main, cols]).astype(BF16)
        y = jnp.dot(d, wp_ref[g], preferred_element_type=F32) * ps_ref[:, cols]
        yp_ref[:, cols] = y.astype(BF16)


def _in_proj_pool(x2, g, w_in, w_pool, pool_scale, cast_a, cast_b, seq, pool_width, attn_width, tm):
    n, d = x2.shape
    in_width = w_in.shape[1]
    ng = len(POOL_WINDOWS)
    gw = pool_width // ng
    hal = IN_HALO
    nh = tm // hal
    steps = n // tm
    resident = dict(pipeline_mode=pl.Buffered(1))
    slab_a = pl.BlockSpec((cast_a.shape[0] // steps, cast_a.shape[1]), lambda i: (i, 0))
    slab_b = pl.BlockSpec((cast_b.shape[0] // steps, cast_b.shape[1]), lambda i: (i, 0))
    return pl.pallas_call(
        functools.partial(_inproj_pool_kernel, tm=tm, seq=seq, pool_width=pool_width, attn_width=attn_width),
        grid=(n // tm,),
        in_specs=[
            pl.BlockSpec((tm, d), lambda i: (i, 0)),
            pl.BlockSpec((hal, d), lambda i: (jnp.maximum(i * nh - 1, 0), 0)),
            pl.BlockSpec((hal, d), lambda i: (jnp.minimum((i + 1) * nh, n // hal - 1), 0)),
            pl.BlockSpec((1, d), lambda i: (0, 0)),
            pl.BlockSpec((d, in_width), lambda i: (0, 0), **resident),
            pl.BlockSpec((ng, gw, gw), lambda i: (0, 0, 0), **resident),
            pl.BlockSpec((1, pool_width), lambda i: (0, 0)),
            slab_a,
            slab_b,
        ],
        out_specs=[
            pl.BlockSpec((tm, pool_width), lambda i: (i, 0)),
            pl.BlockSpec((tm, in_width - pool_width), lambda i: (i, 0)),
            slab_a,
            slab_b,
        ],
        out_shape=[
            jax.ShapeDtypeStruct((n, pool_width), BF16),
            jax.ShapeDtypeStruct((n, in_width - pool_width), BF16),
            jax.ShapeDtypeStruct(cast_a.shape, BF16),
            jax.ShapeDtypeStruct(cast_b.shape, BF16),
        ],
        scratch_shapes=[pltpu.VMEM((tm + 2 * hal, d), BF16)],
        compiler_params=pltpu.CompilerParams(
            dimension_semantics=("parallel",), vmem_limit_bytes=V7X_VMEM_LIMIT),
        name="in_proj_pool",
    )(x2, x2, x2, g, w_in, w_pool, pool_scale, cast_a, cast_b)


def _attn_kernel(sink_ref, q_ref, k_ref, v_ref, ca_ref, cb_ref, o_ref, ca_out_ref, cb_out_ref,
                 bias_ref, sinkrow_ref, vt_ref, st_ref, e_ref, *, tq, seq, slopes):
    ca_out_ref[...] = ca_ref[...].astype(BF16)
    cb_out_ref[...] = cb_ref[...].astype(BF16)
    bi = pl.program_id(0)
    i = pl.program_id(1)
    grp = Q_PER_KV
    n_heads = len(slopes)
    n_kv = n_heads // grp

    @pl.when((bi == 0) & (i == 0))
    def _():
        kj = lax.broadcasted_iota(jnp.int32, (KWIN, QBLK), 0)
        qi = lax.broadcasted_iota(jnp.int32, (KWIN, QBLK), 1)
        for v in range(3):
            dist_i = jnp.abs(v * QBLK + qi - kj)
            valid = dist_i <= WINDOW
            dist = dist_i.astype(F32)
            for h in range(n_heads):
                cols = slice((h % grp) * QBLK, (h % grp + 1) * QBLK)
                bias_ref[v, h // grp, :, cols] = jnp.where(valid, -(slopes[h] * dist) * LOG2E, -jnp.inf)
        for h in range(n_heads):
            cols = slice((h % grp) * QBLK, (h % grp + 1) * QBLK)
            sinkrow_ref[h // grp, :, cols] = jnp.full((1, QBLK), sink_ref[h] * LOG2E, F32)

    @pl.when(i == 0)
    def _():
        def xpose(cb, carry):
            blk = pl.ds(pl.multiple_of(cb * QBLK, QBLK), QBLK)
            vt_ref[:, blk] = v_ref[0, blk, :].T
            return carry
        lax.fori_loop(0, seq // QBLK, xpose, 0)

    def sub_block(j, carry):
        q0 = i * tq + j * QBLK
        k0 = pl.multiple_of(jnp.clip(q0 - WINDOW, 0, seq - KWIN), QBLK)
        variant = (q0 - k0) // QBLK
        rows = pl.ds(pl.multiple_of(j * QBLK, QBLK), QBLK)
        par = (i + j) % 2
        m_all = []
        for hk in range(n_kv):
            q4 = q_ref[0, rows, hk * grp * HEAD_DIM:(hk + 1) * grp * HEAD_DIM]
            q = jnp.concatenate([q4[:, g * HEAD_DIM:(g + 1) * HEAD_DIM] for g in range(grp)], axis=0)
            k = k_ref[0, pl.ds(k0, KWIN), hk * HEAD_DIM:(hk + 1) * HEAD_DIM]
            st = lax.dot_general(k, q, (((1,), (1,)), ((), ())), preferred_element_type=F32)
            st_ref[par, hk] = st + bias_ref[variant, hk]
            m_all.append(jnp.maximum(jnp.max(st_ref[par, hk], axis=0, keepdims=True), sinkrow_ref[hk]))
        for hk in range(n_kv):
            e_ref[par, hk] = jnp.exp2(st_ref[par, hk] - m_all[hk]).astype(BF16)
        ones = jnp.ones((SUM_ROWS, KWIN), BF16)
        for hk in range(n_kv):
            vt = jnp.concatenate([vt_ref[hk * HEAD_DIM:(hk + 1) * HEAD_DIM, pl.ds(k0, KWIN)], ones], axis=0)
            ot = jnp.dot(vt, e_ref[par, hk], preferred_element_type=F32)
            den = ot[HEAD_DIM:HEAD_DIM + 1, :] + jnp.exp2(sinkrow_ref[hk] - m_all[hk])
            ot = ot[:HEAD_DIM, :] * (1.0 / den)
            o4 = jnp.concatenate([ot[:, g * QBLK:(g + 1) * QBLK].T for g in range(grp)], axis=1)
            o_ref[0, rows, hk * grp * HEAD_DIM:(hk + 1) * grp * HEAD_DIM] = o4.astype(BF16)
        return carry

    for j in range(tq // QBLK):
        sub_block(j, 0)


def _attention(qkv3, sink, cast_a, cast_b, attn_width, kv_width, tq):
    b, s, _ = qkv3.shape
    steps = b * (s // tq)
    slab_a = pl.BlockSpec((cast_a.shape[0] // steps, cast_a.shape[1]), lambda bi, i: (bi * (s // tq) + i, 0))
    slab_b = pl.BlockSpec((cast_b.shape[0] // steps, cast_b.shape[1]), lambda bi, i: (bi * (s // tq) + i, 0))
    n_heads = attn_width // HEAD_DIM
    n_kv = n_heads // Q_PER_KV
    idx = np.arange(1, n_heads + 1, dtype=np.float32)
    slopes = tuple(float(v) for v in np.power(2.0, -8.0 * idx / n_heads).astype(np.float32))
    kblk = attn_width // kv_width
    return pl.pallas_call(
        functools.partial(_attn_kernel, tq=tq, seq=s, slopes=slopes),
        grid=(b, s // tq),
        in_specs=[
            pl.BlockSpec(memory_space=pltpu.SMEM),
            pl.BlockSpec((1, tq, attn_width), lambda bi, i: (bi, i, 0)),
            pl.BlockSpec((1, s, kv_width), lambda bi, i: (bi, 0, kblk)),
            pl.BlockSpec((1, s, kv_width), lambda bi, i: (bi, 0, kblk + 1)),
            slab_a,
            slab_b,
        ],
        out_specs=[pl.BlockSpec((1, tq, attn_width), lambda bi, i: (bi, i, 0)), slab_a, slab_b],
        out_shape=[jax.ShapeDtypeStruct((b, s, attn_width), BF16),
                   jax.ShapeDtypeStruct(cast_a.shape, BF16),
                   jax.ShapeDtypeStruct(cast_b.shape, BF16)],
        scratch_shapes=[
            pltpu.VMEM((3, n_kv, KWIN, Q_PER_KV * QBLK), F32),
            pltpu.VMEM((n_kv, 1, Q_PER_KV * QBLK), F32),
            pltpu.VMEM((kv_width, s), BF16),
            pltpu.VMEM((2, n_kv, KWIN, Q_PER_KV * QBLK), F32),
            pltpu.VMEM((2, n_kv, KWIN, Q_PER_KV * QBLK), BF16),
        ],
        compiler_params=pltpu.CompilerParams(
            dimension_semantics=("arbitrary", "arbitrary"), vmem_limit_bytes=V7X_VMEM_LIMIT),
        name="window_attn",
    )(sink, qkv3, qkv3, qkv3, cast_a, cast_b)


def _outproj_kernel(yp_ref, ya_ref, w_ref, x_ref, g_ref, o_ref, *, pool_width, tm):
    half = tm // 2
    for r0 in (0, half):
        rows = slice(r0, r0 + half)
        mix = jnp.dot(yp_ref[rows, :], w_ref[:pool_width, :], preferred_element_type=F32)
        mix = mix + jnp.dot(ya_ref[rows, :], w_ref[pool_width:, :], preferred_element_type=F32)
        o_ref[rows, :] = x_ref[rows, :] + _rms(mix, g_ref[...])


def _out_proj(yp, ya, w_out, x2, g, tm):
    n, d = x2.shape
    pw = yp.shape[1]
    aw = ya.shape[1]
    return pl.pallas_call(
        functools.partial(_outproj_kernel, pool_width=pw, tm=tm),
        grid=(n // tm,),
        in_specs=[
            pl.BlockSpec((tm, pw), lambda i: (i, 0)),
            pl.BlockSpec((tm, aw), lambda i: (i, 0)),
            pl.BlockSpec((pw + aw, d), lambda i: (0, 0), pipeline_mode=pl.Buffered(1)),
            pl.BlockSpec((tm, d), lambda i: (i, 0)),
            pl.BlockSpec((1, d), lambda i: (0, 0)),
        ],
        out_specs=pl.BlockSpec((tm, d), lambda i: (i, 0)),
        out_shape=jax.ShapeDtypeStruct((n, d), F32),
        compiler_params=pltpu.CompilerParams(
            dimension_semantics=("parallel",), vmem_limit_bytes=V7X_VMEM_LIMIT),
        name="out_proj",
    )(yp, ya, w_out, x2, g)


def _gelu_tanh(x):
    return 0.5 * x * (1.0 + jnp.tanh(np.sqrt(2.0 / np.pi).astype(np.float32) * (x + 0.044715 * (x * x * x))))


def _ffn_kernel(x_ref, xp_ref, xn_ref, gpre_ref, wg_ref, wu_ref, cw_ref, cb_ref, wd_ref, gpost_ref,
                o_ref, f_ref, *, tm, tiles_per_seq):
    i = pl.program_id(0)
    c = pl.program_id(1)
    hal = CONV_HALO

    @pl.when(c == 0)
    def _():
        it = i % tiles_per_seq
        gp = gpre_ref[...]
        f_ref[hal:hal + tm, :] = _rms(x_ref[...], gp).astype(BF16)
        fp = _rms(xp_ref[...], gp)
        fn = _rms(xn_ref[...], gp)
        row = lax.broadcasted_iota(jnp.int32, (hal, 1), 0)
        fp16 = jnp.concatenate([jnp.zeros_like(fp), fp], axis=0)
        fn16 = jnp.concatenate([fn, jnp.zeros_like(fn)], axis=0)
        prev_row = jnp.where(it > 0, hal - 1, -1)
        next_row = jnp.where(it < tiles_per_seq - 1, 0, -1)
        f_ref[0:hal, :] = jnp.where(row == prev_row, fp16, 0.0).astype(BF16)
        f_ref[hal + tm:, :] = jnp.where(row == next_row, fn16, 0.0).astype(BF16)
        o_ref[...] = jnp.zeros_like(o_ref)

    g = jnp.dot(f_ref[...], wg_ref[...], preferred_element_type=F32)
    up = jnp.dot(f_ref[hal:hal + tm, :], wu_ref[...], preferred_element_type=F32)
    cw = cw_ref[...]
    g_prev = pltpu.roll(g, 1, axis=0)[hal:hal + tm, :]
    g_next = pltpu.roll(g, tm + 2 * hal - 1, axis=0)[hal:hal + tm, :]
    gate = g_prev * cw[0:1, :] + g[hal:hal + tm, :] * cw[1:2, :] + g_next * cw[2:3, :] + cb_ref[...]
    hmid = (_gelu_tanh(gate) * up).astype(BF16)
    o_ref[...] += jnp.dot(hmid, wd_ref[...], preferred_element_type=F32)

    @pl.when(c == pl.num_programs(1) - 1)
    def _():
        o_ref[...] = x_ref[...] + _rms(o_ref[...], gpost_ref[...])


def _ffn(x1, g_pre, w_gate, w_up, conv_w, conv_b, w_down, g_post, seq, tm, tc):
    n, d = x1.shape
    n_chunks = w_gate.shape[1] // tc
    hal = CONV_HALO
    nb8 = tm // 8
    tiles_per_seq = seq // tm
    return pl.pallas_call(
        functools.partial(_ffn_kernel, tm=tm, tiles_per_seq=tiles_per_seq),
        grid=(n // tm, n_chunks),
        in_specs=[
            pl.BlockSpec((tm, d), lambda i, c: (i, 0)),
            pl.BlockSpec((8, d), lambda i, c: (jnp.maximum(i * nb8 - 1, 0), 0)),
            pl.BlockSpec((8, d), lambda i, c: (jnp.minimum((i + 1) * nb8, n // 8 - 1), 0)),
            pl.BlockSpec((1, d), lambda i, c: (0, 0)),
            pl.BlockSpec((d, tc), lambda i, c: (0, c)),
            pl.BlockSpec((d, tc), lambda i, c: (0, c)),
            pl.BlockSpec((3, tc), lambda i, c: (0, c)),
            pl.BlockSpec((1, tc), lambda i, c: (0, c)),
            pl.BlockSpec((tc, d), lambda i, c: (c, 0)),
            pl.BlockSpec((1, d), lambda i, c: (0, 0)),
        ],
        out_specs=pl.BlockSpec((tm, d), lambda i, c: (i, 0)),
        out_shape=jax.ShapeDtypeStruct((n, d), F32),
        scratch_shapes=[
            pltpu.VMEM((tm + 2 * hal, d), BF16),
        ],
        compiler_params=pltpu.CompilerParams(
            dimension_semantics=("parallel", "arbitrary"), vmem_limit_bytes=V7X_VMEM_LIMIT),
        name="conv_ffn",
    )(x1, x1, x1, g_pre, w_gate, w_up, conv_w, conv_b, w_down, g_post)


def kernel(x, norm_pre_mix, w_in, w_pool, pool_scale, attn_sink, w_out, norm_post_mix, norm_pre_ffn,
           w_gate, w_up, conv_w, conv_b, w_down, norm_post_ffn):
    b, s, d = x.shape
    depth = w_in.shape[0]
    pool_width = pool_scale.shape[1]
    n_heads = attn_sink.shape[1]
    attn_width = n_heads * HEAD_DIM
    kv_width = (w_in.shape[2] - pool_width - attn_width) // 2
    x2 = x.reshape(b * s, d)
    for l in range(depth):
        y_pool, qkv, w_out_bf, w_gate_bf = _in_proj_pool(
            x2, norm_pre_mix[l][None, :], w_in[l].astype(BF16), w_pool[l].astype(BF16), pool_scale[l][None, :],
            w_out[l], w_gate[l], s, pool_width, attn_width, tm=1024)
        y_attn, w_up_bf, w_down_bf = _attention(qkv.reshape(b, s, -1), attn_sink[l], w_up[l], w_down[l],
                                                attn_width, kv_width, tq=512)
        x2 = _out_proj(y_pool, y_attn.reshape(b * s, attn_width), w_out_bf, x2, norm_post_mix[l][None, :], tm=1024)
        x2 = _ffn(x2, norm_pre_ffn[l][None, :], w_gate_bf, w_up_bf, conv_w[l], conv_b[l][None, :], w_down_bf,
                  norm_post_ffn[l][None, :], seq=s, tm=1024, tc=FFN_CHUNK)
    return x2.reshape(b, s, d)
```

```python
import functools

import jax
import jax.numpy as jnp
import numpy as np
from jax import lax
from jax.experimental import pallas as pl
from jax.experimental.pallas import tpu as pltpu

F32 = jnp.float32
BF16 = jnp.bfloat16

EPS = 1e-6
LOG2E = 1.4426950408889634
POOL_WINDOWS = (2, 4, 8, 16)
HEAD_DIM = 64
Q_PER_KV = 4
WINDOW = 128
QBLK = 128
KWIN = 3 * QBLK
SUM_ROWS = 16
IN_HALO = 16
CONV_HALO = 16
FFN_CHUNK = 512
V7X_VMEM_LIMIT = 62 * 1024 * 1024


def _rms(x, g):
    return x * lax.rsqrt(jnp.mean(x * x, axis=-1, keepdims=True) + EPS) * g


def _roll_rows(a, shift):
    return pltpu.roll(a, shift % a.shape[0], axis=0)


def _inproj_pool_kernel(x_ref, xp_ref, xn_ref, g_ref, w_ref, wp_ref, ps_ref, ca_ref, cb_ref,
                        yp_ref, qkv_ref, ca_out_ref, cb_out_ref, h_ref, *, tm, seq, pool_width, attn_width):
    ca_out_ref[...] = ca_ref[...].astype(BF16)
    cb_out_ref[...] = cb_ref[...].astype(BF16)
    i = pl.program_id(0)
    tiles_per_seq = seq // tm
    it = i % tiles_per_seq
    hal = IN_HALO
    half = tm // 2
    rows = tm + 2 * hal
    gw = pool_width // len(POOL_WINDOWS)
    gp = g_ref[...]

    h_ref[0:hal, :] = jnp.where(it > 0, _rms(xp_ref[...], gp), 0.0).astype(BF16)
    h_ref[hal:hal + half, :] = _rms(x_ref[0:half, :], gp).astype(BF16)
    h_ref[hal + half:hal + tm, :] = _rms(x_ref[half:tm, :], gp).astype(BF16)
    h_ref[hal + tm:, :] = jnp.where(it < tiles_per_seq - 1, _rms(xn_ref[...], gp), 0.0).astype(BF16)

    w_u = w_ref[:, :pool_width]
    u = jnp.concatenate([jnp.dot(h_ref[0:hal + half, :], w_u, preferred_element_type=F32),
                         jnp.dot(h_ref[hal + half:rows, :], w_u, preferred_element_type=F32)], axis=0)
    w_q = w_ref[:, pool_width:pool_width + attn_width]
    w_kv = w_ref[:, pool_width + attn_width:]
    for r0 in (0, half):
        hm = h_ref[hal + r0:hal + r0 + half, :]
        q = jnp.dot(hm, w_q, preferred_element_type=F32)
        qkv_ref[r0:r0 + half, :attn_width] = (q * (HEAD_DIM ** -0.5 * LOG2E)).astype(BF16)
        qkv_ref[r0:r0 + half, attn_width:] = jnp.dot(hm, w_kv, preferred_element_type=F32).astype(BF16)

    a1 = u + _roll_rows(u, 1)
    a1r = a1[:, gw:]
    a2 = _roll_rows(a1r, -1) + _roll_rows(a1r, 1)
    a2r = a2[:, gw:]
    a4 = _roll_rows(a2r, -2) + _roll_rows(a2r, 2)
    a4r = a4[:, gw:]
    a8 = _roll_rows(a4r, -4) + _roll_rows(a4r, 4)
    main = slice(hal, hal + tm)
    wins = (a1[main, :gw], a2[main, :gw], a4[main, :gw], a8[main, :])
    t = it * tm + lax.broadcasted_iota(jnp.int32, (tm, 1), 0)
    for g, w in enumerate(POOL_WINDOWS):
        lo = jnp.maximum(t - w // 2, 0)
        hi = jnp.minimum(t + w // 2, seq)
        inv_cnt = 1.0 / (hi - lo).astype(F32)
        cols = slice(g * gw, (g + 1) * gw)
        d = (wins[g] * inv_cnt - u[main, cols]).astype(BF16)
        y = jnp.dot(d, wp_ref[g], preferred_element_type=F32) * ps_ref[:, cols]
        yp_ref[:, cols] = y.astype(BF16)


def _in_proj_pool(x2, g, w_in, w_pool, pool_scale, cast_a, cast_b, seq, pool_width, attn_width, tm):
    n, d = x2.shape
    in_width = w_in.shape[1]
    ng = len(POOL_WINDOWS)
    gw = pool_width // ng
    hal = IN_HALO
    nh = tm // hal
    steps = n // tm
    resident = dict(pipeline_mode=pl.Buffered(1))
    slab_a = pl.BlockSpec((cast_a.shape[0] // steps, cast_a.shape[1]), lambda i: (i, 0))
    slab_b = pl.BlockSpec((cast_b.shape[0] // steps, cast_b.shape[1]), lambda i: (i, 0))
    return pl.pallas_call(
        functools.partial(_inproj_pool_kernel, tm=tm, seq=seq, pool_width=pool_width, attn_width=attn_width),
        grid=(n // tm,),
        in_specs=[
            pl.BlockSpec((tm, d), lambda i: (i, 0)),
            pl.BlockSpec((hal, d), lambda i: (jnp.maximum(i * nh - 1, 0), 0)),
            pl.BlockSpec((hal, d), lambda i: (jnp.minimum((i + 1) * nh, n // hal - 1), 0)),
            pl.BlockSpec((1, d), lambda i: (0, 0)),
            pl.BlockSpec((d, in_width), lambda i: (0, 0), **resident),
            pl.BlockSpec((ng, gw, gw), lambda i: (0, 0, 0), **resident),
            pl.BlockSpec((1, pool_width), lambda i: (0, 0)),
            slab_a,
            slab_b,
        ],
        out_specs=[
            pl.BlockSpec((tm, pool_width), lambda i: (i, 0)),
            pl.BlockSpec((tm, in_width - pool_width), lambda i: (i, 0)),
            slab_a,
            slab_b,
        ],
        out_shape=[
            jax.ShapeDtypeStruct((n, pool_width), BF16),
            jax.ShapeDtypeStruct((n, in_width - pool_width), BF16),
            jax.ShapeDtypeStruct(cast_a.shape, BF16),
            jax.ShapeDtypeStruct(cast_b.shape, BF16),
        ],
        scratch_shapes=[pltpu.VMEM((tm + 2 * hal, d), BF16)],
        compiler_params=pltpu.CompilerParams(
            dimension_semantics=("parallel",), vmem_limit_bytes=V7X_VMEM_LIMIT),
        name="in_proj_pool",
    )(x2, x2, x2, g, w_in, w_pool, pool_scale, cast_a, cast_b)


def _attn_kernel(sink_ref, q_ref, k_ref, v_ref, ca_ref, cb_ref, o_ref, ca_out_ref, cb_out_ref,
                 bias_ref, sinkrow_ref, vt_ref, st_ref, *, tq, seq, slopes):
    ca_out_ref[...] = ca_ref[...].astype(BF16)
    cb_out_ref[...] = cb_ref[...].astype(BF16)
    bi = pl.program_id(0)
    i = pl.program_id(1)
    grp = Q_PER_KV
    n_heads = len(slopes)
    n_kv = n_heads // grp

    @pl.when((bi == 0) & (i == 0))
    def _():
        kj = lax.broadcasted_iota(jnp.int32, (KWIN, QBLK), 0)
        qi = lax.broadcasted_iota(jnp.int32, (KWIN, QBLK), 1)
        for v in range(3):
            dist_i = jnp.abs(v * QBLK + qi - kj)
            valid = dist_i <= WINDOW
            dist = dist_i.astype(F32)
            for h in range(n_heads):
                cols = slice((h % grp) * QBLK, (h % grp + 1) * QBLK)
                bias_ref[v, h // grp, :, cols] = jnp.where(valid, -(slopes[h] * dist) * LOG2E, -jnp.inf)
        for h in range(n_heads):
            cols = slice((h % grp) * QBLK, (h % grp + 1) * QBLK)
            sinkrow_ref[h // grp, :, cols] = jnp.full((1, QBLK), sink_ref[h] * LOG2E, F32)

    @pl.when(i == 0)
    def _():
        def xpose(cb, carry):
            blk = pl.ds(pl.multiple_of(cb * QBLK, QBLK), QBLK)
            vt_ref[:, blk] = v_ref[0, blk, :].T
            return carry
        lax.fori_loop(0, seq // QBLK, xpose, 0)

    def sub_block(j, carry):
        q0 = i * tq + j * QBLK
        k0 = pl.multiple_of(jnp.clip(q0 - WINDOW, 0, seq - KWIN), QBLK)
        variant = (q0 - k0) // QBLK
        rows = pl.ds(pl.multiple_of(j * QBLK, QBLK), QBLK)
        par = (i + j) % 2
        m_all = []
        for hk in range(n_kv):
            q4 = q_ref[0, rows, hk * grp * HEAD_DIM:(hk + 1) * grp * HEAD_DIM]
            q = jnp.concatenate([q4[:, g * HEAD_DIM:(g + 1) * HEAD_DIM] for g in range(grp)], axis=0)
            k = k_ref[0, pl.ds(k0, KWIN), hk * HEAD_DIM:(hk + 1) * HEAD_DIM]
            st = lax.dot_general(k, q, (((1,), (1,)), ((), ())), preferred_element_type=F32)
            st_ref[par, hk] = st + bias_ref[variant, hk]
            m_all.append(jnp.maximum(jnp.max(st_ref[par, hk], axis=0, keepdims=True), sinkrow_ref[hk]))
        ones = jnp.ones((SUM_ROWS, KWIN), BF16)
        for hk in range(n_kv):
            e = jnp.exp2(st_ref[par, hk] - m_all[hk]).astype(BF16)
            vt = jnp.concatenate([vt_ref[hk * HEAD_DIM:(hk + 1) * HEAD_DIM, pl.ds(k0, KWIN)], ones], axis=0)
            ot = jnp.dot(vt, e, preferred_element_type=F32)
            den = ot[HEAD_DIM:HEAD_DIM + 1, :] + jnp.exp2(sinkrow_ref[hk] - m_all[hk])
            ot = ot[:HEAD_DIM, :] * (1.0 / den)
            o4 = jnp.concatenate([ot[:, g * QBLK:(g + 1) * QBLK].T for g in range(grp)], axis=1)
            o_ref[0, rows, hk * grp * HEAD_DIM:(hk + 1) * grp * HEAD_DIM] = o4.astype(BF16)
        return carry

    for j in range(tq // QBLK):
        sub_block(j, 0)


def _attention(qkv3, sink, cast_a, cast_b, attn_width, kv_width, tq):
    b, s, _ = qkv3.shape
    steps = b * (s // tq)
    slab_a = pl.BlockSpec((cast_a.shape[0] // steps, cast_a.shape[1]), lambda bi, i: (bi * (s // tq) + i, 0))
    slab_b = pl.BlockSpec((cast_b.shape[0] // steps, cast_b.shape[1]), lambda bi, i: (bi * (s // tq) + i, 0))
    n_heads = attn_width // HEAD_DIM
    n_kv = n_heads // Q_PER_KV
    idx = np.arange(1, n_heads + 1, dtype=np.float32)
    slopes = tuple(float(v) for v in np.power(2.0, -8.0 * idx / n_heads).astype(np.float32))
    kblk = attn_width // kv_width
    return pl.pallas_call(
        functools.partial(_attn_kernel, tq=tq, seq=s, slopes=slopes),
        grid=(b, s // tq),
        in_specs=[
            pl.BlockSpec(memory_space=pltpu.SMEM),
            pl.BlockSpec((1, tq, attn_width), lambda bi, i: (bi, i, 0)),
            pl.BlockSpec((1, s, kv_width), lambda bi, i: (bi, 0, kblk)),
            pl.BlockSpec((1, s, kv_width), lambda bi, i: (bi, 0, kblk + 1)),
            slab_a,
            slab_b,
        ],
        out_specs=[pl.BlockSpec((1, tq, attn_width), lambda bi, i: (bi, i, 0)), slab_a, slab_b],
        out_shape=[jax.ShapeDtypeStruct((b, s, attn_width), BF16),
                   jax.ShapeDtypeStruct(cast_a.shape, BF16),
                   jax.ShapeDtypeStruct(cast_b.shape, BF16)],
        scratch_shapes=[
            pltpu.VMEM((3, n_kv, KWIN, Q_PER_KV * QBLK), F32),
            pltpu.VMEM((n_kv, 1, Q_PER_KV * QBLK), F32),
            pltpu.VMEM((kv_width, s), BF16),
            pltpu.VMEM((2, n_kv, KWIN, Q_PER_KV * QBLK), F32),
        ],
        compiler_params=pltpu.CompilerParams(
            dimension_semantics=("arbitrary", "arbitrary"), vmem_limit_bytes=V7X_VMEM_LIMIT),
        name="window_attn",
    )(sink, qkv3, qkv3, qkv3, cast_a, cast_b)


def _outproj_kernel(yp_ref, ya_ref, w_ref, x_ref, g_ref, o_ref, *, pool_width, tm):
    half = tm // 2
    for r0 in (0, half):
        rows = slice(r0, r0 + half)
        mix = jnp.dot(yp_ref[rows, :], w_ref[:pool_width, :], preferred_element_type=F32)
        mix = mix + jnp.dot(ya_ref[rows, :], w_ref[pool_width:, :], preferred_element_type=F32)
        o_ref[rows, :] = x_ref[rows, :] + _rms(mix, g_ref[...])


def _out_proj(yp, ya, w_out, x2, g, tm):
    n, d = x2.shape
    pw = yp.shape[1]
    aw = ya.shape[1]
    return pl.pallas_call(
        functools.partial(_outproj_kernel, pool_width=pw, tm=tm),
        grid=(n // tm,),
        in_specs=[
            pl.BlockSpec((tm, pw), lambda i: (i, 0)),
            pl.BlockSpec((tm, aw), lambda i: (i, 0)),
            pl.BlockSpec((pw + aw, d), lambda i: (0, 0), pipeline_mode=pl.Buffered(1)),
            pl.BlockSpec((tm, d), lambda i: (i, 0)),
            pl.BlockSpec((1, d), lambda i: (0, 0)),
        ],
        out_specs=pl.BlockSpec((tm, d), lambda i: (i, 0)),
        out_shape=jax.ShapeDtypeStruct((n, d), F32),
        compiler_params=pltpu.CompilerParams(
            dimension_semantics=("parallel",), vmem_limit_bytes=V7X_VMEM_LIMIT),
        name="out_proj",
    )(yp, ya, w_out, x2, g)


def _gelu_tanh(x):
    return 0.5 * x * (1.0 + jnp.tanh(np.sqrt(2.0 / np.pi).astype(np.float32) * (x + 0.044715 * (x * x * x))))


def _ffn_kernel(x_ref, xp_ref, xn_ref, gpre_ref, wg_ref, wu_ref, cw_ref, cb_ref, wd_ref, gpost_ref,
                o_ref, f_ref, *, tm, tiles_per_seq):
    i = pl.program_id(0)
    c = pl.program_id(1)
    hal = CONV_HALO

    @pl.when(c == 0)
    def _():
        it = i % tiles_per_seq
        gp = gpre_ref[...]
        f_ref[0:tm, :] = _rms(x_ref[...], gp).astype(BF16)
        edge = jnp.concatenate([_rms(xn_ref[...], gp), _rms(xp_ref[...], gp)], axis=0)
        row = lax.broadcasted_iota(jnp.int32, (hal, 1), 0)
        next_row = jnp.where(it < tiles_per_seq - 1, 0, -1)
        prev_row = jnp.where(it > 0, hal - 1, -1)
        f_ref[tm:, :] = jnp.where((row == next_row) | (row == prev_row), edge, 0.0).astype(BF16)
        o_ref[...] = jnp.zeros_like(o_ref)

    g = jnp.dot(f_ref[...], wg_ref[...], preferred_element_type=F32)
    up = jnp.dot(f_ref[0:tm, :], wu_ref[...], preferred_element_type=F32)
    cw = cw_ref[...]
    g_prev = pltpu.roll(g, 1, axis=0)[0:tm, :]
    g_next = pltpu.roll(g, tm + hal - 1, axis=0)[0:tm, :]
    gate = g_prev * cw[0:1, :] + g[0:tm, :] * cw[1:2, :] + g_next * cw[2:3, :] + cb_ref[...]
    hmid = (_gelu_tanh(gate) * up).astype(BF16)
    o_ref[...] += jnp.dot(hmid, wd_ref[...], preferred_element_type=F32)

    @pl.when(c == pl.num_programs(1) - 1)
    def _():
        o_ref[...] = x_ref[...] + _rms(o_ref[...], gpost_ref[...])


def _ffn(x1, g_pre, w_gate, w_up, conv_w, conv_b, w_down, g_post, seq, tm, tc):
    n, d = x1.shape
    n_chunks = w_gate.shape[1] // tc
    hal = CONV_HALO
    nb8 = tm // 8
    tiles_per_seq = seq // tm
    return pl.pallas_call(
        functools.partial(_ffn_kernel, tm=tm, tiles_per_seq=tiles_per_seq),
        grid=(n // tm, n_chunks),
        in_specs=[
            pl.BlockSpec((tm, d), lambda i, c: (i, 0)),
            pl.BlockSpec((8, d), lambda i, c: (jnp.maximum(i * nb8 - 1, 0), 0)),
            pl.BlockSpec((8, d), lambda i, c: (jnp.minimum((i + 1) * nb8, n // 8 - 1), 0)),
            pl.BlockSpec((1, d), lambda i, c: (0, 0)),
            pl.BlockSpec((d, tc), lambda i, c: (0, c)),
            pl.BlockSpec((d, tc), lambda i, c: (0, c)),
            pl.BlockSpec((3, tc), lambda i, c: (0, c)),
            pl.BlockSpec((1, tc), lambda i, c: (0, c)),
            pl.BlockSpec((tc, d), lambda i, c: (c, 0)),
            pl.BlockSpec((1, d), lambda i, c: (0, 0)),
        ],
        out_specs=pl.BlockSpec((tm, d), lambda i, c: (i, 0)),
        out_shape=jax.ShapeDtypeStruct((n, d), F32),
        scratch_shapes=[
            pltpu.VMEM((tm + hal, d), BF16),
        ],
        compiler_params=pltpu.CompilerParams(
            dimension_semantics=("parallel", "arbitrary"), vmem_limit_bytes=V7X_VMEM_LIMIT),
        name="conv_ffn",
    )(x1, x1, x1, g_pre, w_gate, w_up, conv_w, conv_b, w_down, g_post)


def kernel(x, norm_pre_mix, w_in, w_pool, pool_scale, attn_sink, w_out, norm_post_mix, norm_pre_ffn,
           w_gate, w_up, conv_w, conv_b, w_down, norm_post_ffn):
    b, s, d = x.shape
    depth = w_in.shape[0]
    pool_width = pool_scale.shape[1]
    n_heads = attn_sink.shape[1]
    attn_width = n_heads * HEAD_DIM
    kv_width = (w_in.shape[2] - pool_width - attn_width) // 2
    x2 = x.reshape(b * s, d)
    for l in range(depth):
        y_pool, qkv, w_out_bf, w_gate_bf = _in_proj_pool(
            x2, norm_pre_mix[l][None, :], w_in[l].astype(BF16), w_pool[l].astype(BF16), pool_scale[l][None, :],
            w_out[l], w_gate[l], s, pool_width, attn_width, tm=1024)
        y_attn, w_up_bf, w_down_bf = _attention(qkv.reshape(b, s, -1), attn_sink[l], w_up[l], w_down[l],
                                                attn_width, kv_width, tq=512)
        x2 = _out_proj(y_pool, y_attn.reshape(b * s, attn_width), w_out_bf, x2, norm_post_mix[l][None, :], tm=1024)
        x2 = _ffn(x2, norm_pre_ffn[l][None, :], w_gate_bf, w_up_bf, conv_w[l], conv_b[l][None, :], w_down_bf,
                  norm_post_ffn[l][None, :], seq=s, tm=1024, tc=FFN_CHUNK)
    return x2.reshape(b, s, d)
```

```python
import functools

import jax
import jax.numpy as jnp
import numpy as np
from jax import lax
from jax.experimental import pallas as pl
from jax.experimental.pallas import tpu as pltpu

F32 = jnp.float32
BF16 = jnp.bfloat16

EPS = 1e-6
LOG2E = 1.4426950408889634
POOL_WINDOWS = (2, 4, 8, 16)
HEAD_DIM = 64
Q_PER_KV = 4
WINDOW = 128
QBLK = 128
KWIN = 3 * QBLK
SUM_ROWS = 16
IN_HALO = 16
CONV_HALO = 16
FFN_CHUNK = 512
V7X_VMEM_LIMIT = 62 * 1024 * 1024


def _rms(x, g):
    return x * lax.rsqrt(jnp.mean(x * x, axis=-1, keepdims=True) + EPS) * g


def _roll_rows(a, shift):
    return pltpu.roll(a, shift % a.shape[0], axis=0)


def _inproj_pool_kernel(x_ref, xp_ref, xn_ref, g_ref, w_ref, wp_ref, ps_ref, ca_ref, cb_ref,
                        yp_ref, qt_ref, k_ref, vt_ref, ca_out_ref, cb_out_ref, h_ref,
                        *, tm, seq, pool_width, attn_width):
    ca_out_ref[...] = ca_ref[...].astype(BF16)
    cb_out_ref[...] = cb_ref[...].astype(BF16)
    i = pl.program_id(0)
    tiles_per_seq = seq // tm
    it = i % tiles_per_seq
    hal = IN_HALO
    half = tm // 2
    rows = tm + 2 * hal
    gw = pool_width // len(POOL_WINDOWS)
    gp = g_ref[...]

    h_ref[0:hal, :] = jnp.where(it > 0, _rms(xp_ref[...], gp), 0.0).astype(BF16)
    h_ref[hal:hal + half, :] = _rms(x_ref[0:half, :], gp).astype(BF16)
    h_ref[hal + half:hal + tm, :] = _rms(x_ref[half:tm, :], gp).astype(BF16)
    h_ref[hal + tm:, :] = jnp.where(it < tiles_per_seq - 1, _rms(xn_ref[...], gp), 0.0).astype(BF16)

    w_u = w_ref[:, :pool_width]
    u = jnp.concatenate([jnp.dot(h_ref[0:hal + half, :], w_u, preferred_element_type=F32),
                         jnp.dot(h_ref[hal + half:rows, :], w_u, preferred_element_type=F32)], axis=0)
    w_q = w_ref[:, pool_width:pool_width + attn_width]
    w_kv = w_ref[:, pool_width + attn_width:]
    kv_width = k_ref.shape[1]
    grp_width = Q_PER_KV * HEAD_DIM
    for r0 in (0, half):
        hm = h_ref[hal + r0:hal + r0 + half, :]
        q = jnp.dot(hm, w_q, preferred_element_type=F32) * (HEAD_DIM ** -0.5 * LOG2E)
        for jb in range(half // QBLK):
            for hk in range(attn_width // grp_width):
                t = q[jb * QBLK:(jb + 1) * QBLK, hk * grp_width:(hk + 1) * grp_width].T
                qt = jnp.concatenate([t[g * HEAD_DIM:(g + 1) * HEAD_DIM, :] for g in range(Q_PER_KV)], axis=1)
                qt_ref[r0 // QBLK + jb, hk] = qt.astype(BF16)
        kv = jnp.dot(hm, w_kv, preferred_element_type=F32)
        k_ref[r0:r0 + half, :] = kv[:, :kv_width].astype(BF16)
        vt_ref[:, r0:r0 + half] = kv[:, kv_width:].T.astype(BF16)

    a1 = u + _roll_rows(u, 1)
    a1r = a1[:, gw:]
    a2 = _roll_rows(a1r, -1) + _roll_rows(a1r, 1)
    a2r = a2[:, gw:]
    a4 = _roll_rows(a2r, -2) + _roll_rows(a2r, 2)
    a4r = a4[:, gw:]
    a8 = _roll_rows(a4r, -4) + _roll_rows(a4r, 4)
    main = slice(hal, hal + tm)
    wins = (a1[main, :gw], a2[main, :gw], a4[main, :gw], a8[main, :])
    t = it * tm + lax.broadcasted_iota(jnp.int32, (tm, 1), 0)
    for g, w in enumerate(POOL_WINDOWS):
        lo = jnp.maximum(t - w // 2, 0)
        hi = jnp.minimum(t + w // 2, seq)
        inv_cnt = 1.0 / (hi - lo).astype(F32)
        cols = slice(g * gw, (g + 1) * gw)
        d = (wins[g] * inv_cnt - u[main, cols]).astype(BF16)
        y = jnp.dot(d, wp_ref[g], preferred_element_type=F32) * ps_ref[:, cols]
        yp_ref[:, cols] = y.astype(BF16)


def _in_proj_pool(x2, g, w_in, w_pool, pool_scale, cast_a, cast_b, seq, pool_width, attn_width, tm):
    n, d = x2.shape
    in_width = w_in.shape[1]
    ng = len(POOL_WINDOWS)
    gw = pool_width // ng
    hal = IN_HALO
    nh = tm // hal
    steps = n // tm
    kv_width = (in_width - pool_width - attn_width) // 2
    n_kv = attn_width // (Q_PER_KV * HEAD_DIM)
    resident = dict(pipeline_mode=pl.Buffered(1))
    slab_a = pl.BlockSpec((cast_a.shape[0] // steps, cast_a.shape[1]), lambda i: (i, 0))
    slab_b = pl.BlockSpec((cast_b.shape[0] // steps, cast_b.shape[1]), lambda i: (i, 0))
    return pl.pallas_call(
        functools.partial(_inproj_pool_kernel, tm=tm, seq=seq, pool_width=pool_width, attn_width=attn_width),
        grid=(n // tm,),
        in_specs=[
            pl.BlockSpec((tm, d), lambda i: (i, 0)),
            pl.BlockSpec((hal, d), lambda i: (jnp.maximum(i * nh - 1, 0), 0)),
            pl.BlockSpec((hal, d), lambda i: (jnp.minimum((i + 1) * nh, n // hal - 1), 0)),
            pl.BlockSpec((1, d), lambda i: (0, 0)),
            pl.BlockSpec((d, in_width), lambda i: (0, 0), **resident),
            pl.BlockSpec((ng, gw, gw), lambda i: (0, 0, 0), **resident),
            pl.BlockSpec((1, pool_width), lambda i: (0, 0)),
            slab_a,
            slab_b,
        ],
        out_specs=[
            pl.BlockSpec((tm, pool_width), lambda i: (i, 0)),
            pl.BlockSpec((tm // QBLK, n_kv, HEAD_DIM, Q_PER_KV * QBLK), lambda i: (i, 0, 0, 0)),
            pl.BlockSpec((tm, kv_width), lambda i: (i, 0)),
            pl.BlockSpec((kv_width, tm), lambda i: (0, i)),
            slab_a,
            slab_b,
        ],
        out_shape=[
            jax.ShapeDtypeStruct((n, pool_width), BF16),
            jax.ShapeDtypeStruct((n // QBLK, n_kv, HEAD_DIM, Q_PER_KV * QBLK), BF16),
            jax.ShapeDtypeStruct((n, kv_width), BF16),
            jax.ShapeDtypeStruct((kv_width, n), BF16),
            jax.ShapeDtypeStruct(cast_a.shape, BF16),
            jax.ShapeDtypeStruct(cast_b.shape, BF16),
        ],
        scratch_shapes=[pltpu.VMEM((tm + 2 * hal, d), BF16)],
        compiler_params=pltpu.CompilerParams(
            dimension_semantics=("parallel",), vmem_limit_bytes=V7X_VMEM_LIMIT),
        name="in_proj_pool",
    )(x2, x2, x2, g, w_in, w_pool, pool_scale, cast_a, cast_b)


def _attn_kernel(sink_ref, qt_ref, k_ref, vt_ref, ca_ref, cb_ref, o_ref, ca_out_ref, cb_out_ref,
                 bias_ref, sinkrow_ref, st_ref, *, tq, seq, slopes):
    ca_out_ref[...] = ca_ref[...].astype(BF16)
    cb_out_ref[...] = cb_ref[...].astype(BF16)
    bi = pl.program_id(0)
    i = pl.program_id(1)
    grp = Q_PER_KV
    n_heads = len(slopes)
    n_kv = n_heads // grp

    @pl.when((bi == 0) & (i == 0))
    def _():
        kj = lax.broadcasted_iota(jnp.int32, (KWIN, QBLK), 0)
        qi = lax.broadcasted_iota(jnp.int32, (KWIN, QBLK), 1)
        for v in range(3):
            dist_i = jnp.abs(v * QBLK + qi - kj)
            valid = dist_i <= WINDOW
            dist = dist_i.astype(F32)
            for h in range(n_heads):
                cols = slice((h % grp) * QBLK, (h % grp + 1) * QBLK)
                bias_ref[v, h // grp, :, cols] = jnp.where(valid, -(slopes[h] * dist) * LOG2E, -jnp.inf)
        for h in range(n_heads):
            cols = slice((h % grp) * QBLK, (h % grp + 1) * QBLK)
            sinkrow_ref[h // grp, :, cols] = jnp.full((1, QBLK), sink_ref[h] * LOG2E, F32)

    def sub_block(j, carry):
        q0 = i * tq + j * QBLK
        k0 = pl.multiple_of(jnp.clip(q0 - WINDOW, 0, seq - KWIN), QBLK)
        variant = (q0 - k0) // QBLK
        rows = pl.ds(pl.multiple_of(j * QBLK, QBLK), QBLK)
        par = (i + j) % 2
        m_all = []
        for hk in range(n_kv):
            k = k_ref[0, pl.ds(k0, KWIN), hk * HEAD_DIM:(hk + 1) * HEAD_DIM]
            st = jnp.dot(k, qt_ref[j, hk], preferred_element_type=F32)
            st_ref[par, hk] = st + bias_ref[variant, hk]
            m_all.append(jnp.maximum(jnp.max(st_ref[par, hk], axis=0, keepdims=True), sinkrow_ref[hk]))
        ones = jnp.ones((SUM_ROWS, KWIN), BF16)
        for hk in range(n_kv):
            e = jnp.exp2(st_ref[par, hk] - m_all[hk]).astype(BF16)
            vt = jnp.concatenate([vt_ref[hk * HEAD_DIM:(hk + 1) * HEAD_DIM, pl.ds(k0, KWIN)], ones], axis=0)
            ot = jnp.dot(vt, e, preferred_element_type=F32)
            den = ot[HEAD_DIM:HEAD_DIM + 1, :] + jnp.exp2(sinkrow_ref[hk] - m_all[hk])
            ot = ot[:HEAD_DIM, :] * (1.0 / den)
            o4 = jnp.concatenate([ot[:, g * QBLK:(g + 1) * QBLK].T for g in range(grp)], axis=1)
            o_ref[0, rows, hk * grp * HEAD_DIM:(hk + 1) * grp * HEAD_DIM] = o4.astype(BF16)
        return carry

    for j in range(tq // QBLK):
        sub_block(j, 0)


def _attention(qt, k3, vt, sink, cast_a, cast_b, tq):
    b, s, kv_width = k3.shape
    n_kv = qt.shape[1]
    attn_width = n_kv * Q_PER_KV * HEAD_DIM
    steps = b * (s // tq)
    slab_a = pl.BlockSpec((cast_a.shape[0] // steps, cast_a.shape[1]), lambda bi, i: (bi * (s // tq) + i, 0))
    slab_b = pl.BlockSpec((cast_b.shape[0] // steps, cast_b.shape[1]), lambda bi, i: (bi * (s // tq) + i, 0))
    n_heads = attn_width // HEAD_DIM
    idx = np.arange(1, n_heads + 1, dtype=np.float32)
    slopes = tuple(float(v) for v in np.power(2.0, -8.0 * idx / n_heads).astype(np.float32))
    return pl.pallas_call(
        functools.partial(_attn_kernel, tq=tq, seq=s, slopes=slopes),
        grid=(b, s // tq),
        in_specs=[
            pl.BlockSpec(memory_space=pltpu.SMEM),
            pl.BlockSpec((tq // QBLK, n_kv, HEAD_DIM, Q_PER_KV * QBLK), lambda bi, i: (bi * (s // tq) + i, 0, 0, 0)),
            pl.BlockSpec((1, s, kv_width), lambda bi, i: (bi, 0, 0)),
            pl.BlockSpec((kv_width, s), lambda bi, i: (0, bi)),
            slab_a,
            slab_b,
        ],
        out_specs=[pl.BlockSpec((1, tq, attn_width), lambda bi, i: (bi, i, 0)), slab_a, slab_b],
        out_shape=[jax.ShapeDtypeStruct((b, s, attn_width), BF16),
                   jax.ShapeDtypeStruct(cast_a.shape, BF16),
                   jax.ShapeDtypeStruct(cast_b.shape, BF16)],
        scratch_shapes=[
            pltpu.VMEM((3, n_kv, KWIN, Q_PER_KV * QBLK), F32),
            pltpu.VMEM((n_kv, 1, Q_PER_KV * QBLK), F32),
            pltpu.VMEM((2, n_kv, KWIN, Q_PER_KV * QBLK), F32),
        ],
        compiler_params=pltpu.CompilerParams(
            dimension_semantics=("arbitrary", "arbitrary"), vmem_limit_bytes=V7X_VMEM_LIMIT),
        name="window_attn",
    )(sink, qt, k3, vt, cast_a, cast_b)


def _outproj_kernel(yp_ref, ya_ref, w_ref, x_ref, g_ref, o_ref, *, pool_width, tm):
    half = tm // 2
    for r0 in (0, half):
        rows = slice(r0, r0 + half)
        mix = jnp.dot(yp_ref[rows, :], w_ref[:pool_width, :], preferred_element_type=F32)
        mix = mix + jnp.dot(ya_ref[rows, :], w_ref[pool_width:, :], preferred_element_type=F32)
        o_ref[rows, :] = x_ref[rows, :] + _rms(mix, g_ref[...])


def _out_proj(yp, ya, w_out, x2, g, tm):
    n, d = x2.shape
    pw = yp.shape[1]
    aw = ya.shape[1]
    return pl.pallas_call(
        functools.partial(_outproj_kernel, pool_width=pw, tm=tm),
        grid=(n // tm,),
        in_specs=[
            pl.BlockSpec((tm, pw), lambda i: (i, 0)),
            pl.BlockSpec((tm, aw), lambda i: (i, 0)),
            pl.BlockSpec((pw + aw, d), lambda i: (0, 0), pipeline_mode=pl.Buffered(1)),
            pl.BlockSpec((tm, d), lambda i: (i, 0)),
            pl.BlockSpec((1, d), lambda i: (0, 0)),
        ],
        out_specs=pl.BlockSpec((tm, d), lambda i: (i, 0)),
        out_shape=jax.ShapeDtypeStruct((n, d), F32),
        compiler_params=pltpu.CompilerParams(
            dimension_semantics=("parallel",), vmem_limit_bytes=V7X_VMEM_LIMIT),
        name="out_proj",
    )(yp, ya, w_out, x2, g)


def _gelu_tanh(x):
    return 0.5 * x * (1.0 + jnp.tanh(np.sqrt(2.0 / np.pi).astype(np.float32) * (x + 0.044715 * (x * x * x))))


def _ffn_kernel(x_ref, xp_ref, xn_ref, gpre_ref, wg_ref, wu_ref, cw_ref, cb_ref, wd_ref, gpost_ref,
                o_ref, f_ref, *, tm, tiles_per_seq):
    i = pl.program_id(0)
    c = pl.program_id(1)
    hal = CONV_HALO

    @pl.when(c == 0)
    def _():
        it = i % tiles_per_seq
        gp = gpre_ref[...]
        f_ref[0:tm, :] = _rms(x_ref[...], gp).astype(BF16)
        edge = jnp.concatenate([_rms(xn_ref[...], gp), _rms(xp_ref[...], gp)], axis=0)
        row = lax.broadcasted_iota(jnp.int32, (hal, 1), 0)
        next_row = jnp.where(it < tiles_per_seq - 1, 0, -1)
        prev_row = jnp.where(it > 0, hal - 1, -1)
        f_ref[tm:, :] = jnp.where((row == next_row) | (row == prev_row), edge, 0.0).astype(BF16)
        o_ref[...] = jnp.zeros_like(o_ref)

    g = jnp.dot(f_ref[...], wg_ref[...], preferred_element_type=F32)
    up = jnp.dot(f_ref[0:tm, :], wu_ref[...], preferred_element_type=F32)
    cw = cw_ref[...]
    g_prev = pltpu.roll(g, 1, axis=0)[0:tm, :]
    g_next = pltpu.roll(g, tm + hal - 1, axis=0)[0:tm, :]
    gate = g_prev * cw[0:1, :] + g[0:tm, :] * cw[1:2, :] + g_next * cw[2:3, :] + cb_ref[...]
    hmid = (_gelu_tanh(gate) * up).astype(BF16)
    o_ref[...] += jnp.dot(hmid, wd_ref[...], preferred_element_type=F32)

    @pl.when(c == pl.num_programs(1) - 1)
    def _():
        o_ref[...] = x_ref[...] + _rms(o_ref[...], gpost_ref[...])


def _ffn(x1, g_pre, w_gate, w_up, conv_w, conv_b, w_down, g_post, seq, tm, tc):
    n, d = x1.shape
    n_chunks = w_gate.shape[1] // tc
    hal = CONV_HALO
    nb8 = tm // 8
    tiles_per_seq = seq // tm
    return pl.pallas_call(
        functools.partial(_ffn_kernel, tm=tm, tiles_per_seq=tiles_per_seq),
        grid=(n // tm, n_chunks),
        in_specs=[
            pl.BlockSpec((tm, d), lambda i, c: (i, 0)),
            pl.BlockSpec((8, d), lambda i, c: (jnp.maximum(i * nb8 - 1, 0), 0)),
            pl.BlockSpec((8, d), lambda i, c: (jnp.minimum((i + 1) * nb8, n // 8 - 1), 0)),
            pl.BlockSpec((1, d), lambda i, c: (0, 0)),
            pl.BlockSpec((d, tc), lambda i, c: (0, c)),
            pl.BlockSpec((d, tc), lambda i, c: (0, c)),
            pl.BlockSpec((3, tc), lambda i, c: (0, c)),
            pl.BlockSpec((1, tc), lambda i, c: (0, c)),
            pl.BlockSpec((tc, d), lambda i, c: (c, 0)),
            pl.BlockSpec((1, d), lambda i, c: (0, 0)),
        ],
        out_specs=pl.BlockSpec((tm, d), lambda i, c: (i, 0)),
        out_shape=jax.ShapeDtypeStruct((n, d), F32),
        scratch_shapes=[
            pltpu.VMEM((tm + hal, d), BF16),
        ],
        compiler_params=pltpu.CompilerParams(
            dimension_semantics=("parallel", "arbitrary"), vmem_limit_bytes=V7X_VMEM_LIMIT),
        name="conv_ffn",
    )(x1, x1, x1, g_pre, w_gate, w_up, conv_w, conv_b, w_down, g_post)


def kernel(x, norm_pre_mix, w_in, w_pool, pool_scale, attn_sink, w_out, norm_post_mix, norm_pre_ffn,
           w_gate, w_up, conv_w, conv_b, w_down, norm_post_ffn):
    b, s, d = x.shape
    depth = w_in.shape[0]
    pool_width = pool_scale.shape[1]
    n_heads = attn_sink.shape[1]
    attn_width = n_heads * HEAD_DIM
    x2 = x.reshape(b * s, d)
    for l in range(depth):
        y_pool, qt, k, vt, w_out_bf, w_gate_bf = _in_proj_pool(
            x2, norm_pre_mix[l][None, :], w_in[l].astype(BF16), w_pool[l].astype(BF16), pool_scale[l][None, :],
            w_out[l], w_gate[l], s, pool_width, attn_width, tm=1024)
        y_attn, w_up_bf, w_down_bf = _attention(qt, k.reshape(b, s, -1), vt, attn_sink[l], w_up[l], w_down[l],
                                                tq=1024)
        x2 = _out_proj(y_pool, y_attn.reshape(b * s, attn_width), w_out_bf, x2, norm_post_mix[l][None, :], tm=1024)
        x2 = _ffn(x2, norm_pre_ffn[l][None, :], w_gate_bf, w_up_bf, conv_w[l], conv_b[l][None, :], w_down_bf,
                  norm_post_ffn[l][None, :], seq=s, tm=1024, tc=FFN_CHUNK)
    return x2.reshape(b, s, d)
```

```python
import functools

import jax
import jax.numpy as jnp
import numpy as np
from jax import lax
from jax.experimental import pallas as pl
from jax.experimental.pallas import tpu as pltpu

F32 = jnp.float32
BF16 = jnp.bfloat16

EPS = 1e-6
LOG2E = 1.4426950408889634
POOL_WINDOWS = (2, 4, 8, 16)
HEAD_DIM = 64
Q_PER_KV = 4
WINDOW = 128
QBLK = 128
KWIN = 3 * QBLK
SUM_ROWS = 16
IN_HALO = 16
CONV_HALO = 16
FFN_CHUNK = 512
V7X_VMEM_LIMIT = 62 * 1024 * 1024


def _rms(x, g):
    return x * lax.rsqrt(jnp.mean(x * x, axis=-1, keepdims=True) + EPS) * g


def _roll_rows(a, shift):
    return pltpu.roll(a, shift % a.shape[0], axis=0)


def _inproj_pool_kernel(x_ref, xp_ref, xn_ref, g_ref, w_ref, wp_ref, ps_ref, ca_ref, cb_ref,
                        yp_ref, qt_ref, k_ref, vt_ref, ca_out_ref, cb_out_ref, h_ref,
                        *, tm, seq, pool_width, attn_width):
    ca_out_ref[...] = ca_ref[...].astype(BF16)
    cb_out_ref[...] = cb_ref[...].astype(BF16)
    i = pl.program_id(0)
    tiles_per_seq = seq // tm
    it = i % tiles_per_seq
    hal = IN_HALO
    half = tm // 2
    rows = tm + 2 * hal
    gw = pool_width // len(POOL_WINDOWS)
    gp = g_ref[...]

    h_ref[0:hal, :] = jnp.where(it > 0, _rms(xp_ref[...], gp), 0.0).astype(BF16)
    h_ref[hal:hal + half, :] = _rms(x_ref[0:half, :], gp).astype(BF16)
    h_ref[hal + half:hal + tm, :] = _rms(x_ref[half:tm, :], gp).astype(BF16)
    h_ref[hal + tm:, :] = jnp.where(it < tiles_per_seq - 1, _rms(xn_ref[...], gp), 0.0).astype(BF16)

    w_u = w_ref[:, :pool_width]
    u = jnp.concatenate([jnp.dot(h_ref[0:hal + half, :], w_u, preferred_element_type=F32),
                         jnp.dot(h_ref[hal + half:rows, :], w_u, preferred_element_type=F32)], axis=0)
    w_q = w_ref[:, pool_width:pool_width + attn_width]
    w_kv = w_ref[:, pool_width + attn_width:]
    kv_width = k_ref.shape[1]
    grp_width = Q_PER_KV * HEAD_DIM
    for r0 in (0, half):
        hm = h_ref[hal + r0:hal + r0 + half, :]
        q = jnp.dot(hm, w_q, preferred_element_type=F32) * (HEAD_DIM ** -0.5 * LOG2E)
        for jb in range(half // QBLK):
            for hk in range(attn_width // grp_width):
                t = q[jb * QBLK:(jb + 1) * QBLK, hk * grp_width:(hk + 1) * grp_width].T
                qt = jnp.concatenate([t[g * HEAD_DIM:(g + 1) * HEAD_DIM, :] for g in range(Q_PER_KV)], axis=1)
                qt_ref[r0 // QBLK + jb, hk] = qt.astype(BF16)
        kv = jnp.dot(hm, w_kv, preferred_element_type=F32)
        k_ref[r0:r0 + half, :] = kv[:, :kv_width].astype(BF16)
        vt_ref[:, r0:r0 + half] = kv[:, kv_width:].T.astype(BF16)

    a1 = u + _roll_rows(u, 1)
    a1r = a1[:, gw:]
    a2 = _roll_rows(a1r, -1) + _roll_rows(a1r, 1)
    a2r = a2[:, gw:]
    a4 = _roll_rows(a2r, -2) + _roll_rows(a2r, 2)
    a4r = a4[:, gw:]
    a8 = _roll_rows(a4r, -4) + _roll_rows(a4r, 4)
    main = slice(hal, hal + tm)
    wins = (a1[main, :gw], a2[main, :gw], a4[main, :gw], a8[main, :])
    t = it * tm + lax.broadcasted_iota(jnp.int32, (tm, 1), 0)
    for g, w in enumerate(POOL_WINDOWS):
        lo = jnp.maximum(t - w // 2, 0)
        hi = jnp.minimum(t + w // 2, seq)
        inv_cnt = 1.0 / (hi - lo).astype(F32)
        cols = slice(g * gw, (g + 1) * gw)
        d = (wins[g] * inv_cnt - u[main, cols]).astype(BF16)
        y = jnp.dot(d, wp_ref[g], preferred_element_type=F32) * ps_ref[:, cols]
        yp_ref[:, cols] = y.astype(BF16)


def _in_proj_pool(x2, g, w_in, w_pool, pool_scale, cast_a, cast_b, seq, pool_width, attn_width, tm):
    n, d = x2.shape
    in_width = w_in.shape[1]
    ng = len(POOL_WINDOWS)
    gw = pool_width // ng
    hal = IN_HALO
    nh = tm // hal
    steps = n // tm
    kv_width = (in_width - pool_width - attn_width) // 2
    n_kv = attn_width // (Q_PER_KV * HEAD_DIM)
    resident = dict(pipeline_mode=pl.Buffered(1))
    slab_a = pl.BlockSpec((cast_a.shape[0] // steps, cast_a.shape[1]), lambda i: (i, 0))
    slab_b = pl.BlockSpec((cast_b.shape[0] // steps, cast_b.shape[1]), lambda i: (i, 0))
    return pl.pallas_call(
        functools.partial(_inproj_pool_kernel, tm=tm, seq=seq, pool_width=pool_width, attn_width=attn_width),
        grid=(n // tm,),
        in_specs=[
            pl.BlockSpec((tm, d), lambda i: (i, 0)),
            pl.BlockSpec((hal, d), lambda i: (jnp.maximum(i * nh - 1, 0), 0)),
            pl.BlockSpec((hal, d), lambda i: (jnp.minimum((i + 1) * nh, n // hal - 1), 0)),
            pl.BlockSpec((1, d), lambda i: (0, 0)),
            pl.BlockSpec((d, in_width), lambda i: (0, 0), **resident),
            pl.BlockSpec((ng, gw, gw), lambda i: (0, 0, 0), **resident),
            pl.BlockSpec((1, pool_width), lambda i: (0, 0)),
            slab_a,
            slab_b,
        ],
        out_specs=[
            pl.BlockSpec((tm, pool_width), lambda i: (i, 0)),
            pl.BlockSpec((tm // QBLK, n_kv, HEAD_DIM, Q_PER_KV * QBLK), lambda i: (i, 0, 0, 0)),
            pl.BlockSpec((tm, kv_width), lambda i: (i, 0)),
            pl.BlockSpec((kv_width, tm), lambda i: (0, i)),
            slab_a,
            slab_b,
        ],
        out_shape=[
            jax.ShapeDtypeStruct((n, pool_width), BF16),
            jax.ShapeDtypeStruct((n // QBLK, n_kv, HEAD_DIM, Q_PER_KV * QBLK), BF16),
            jax.ShapeDtypeStruct((n, kv_width), BF16),
            jax.ShapeDtypeStruct((kv_width, n), BF16),
            jax.ShapeDtypeStruct(cast_a.shape, BF16),
            jax.ShapeDtypeStruct(cast_b.shape, BF16),
        ],
        scratch_shapes=[pltpu.VMEM((tm + 2 * hal, d), BF16)],
        compiler_params=pltpu.CompilerParams(
            dimension_semantics=("parallel",), vmem_limit_bytes=V7X_VMEM_LIMIT),
        name="in_proj_pool",
    )(x2, x2, x2, g, w_in, w_pool, pool_scale, cast_a, cast_b)


def _attn_kernel(sink_ref, qt_ref, k_ref, vt_ref, ca_ref, cb_ref, o_ref, ca_out_ref, cb_out_ref,
                 bias_ref, sinkrow_ref, st_ref, *, tq, seq, slopes):
    ca_out_ref[...] = ca_ref[...].astype(BF16)
    cb_out_ref[...] = cb_ref[...].astype(BF16)
    bi = pl.program_id(0)
    i = pl.program_id(1)
    grp = Q_PER_KV
    n_heads = len(slopes)
    n_kv = n_heads // grp

    @pl.when((bi == 0) & (i == 0))
    def _():
        kj = lax.broadcasted_iota(jnp.int32, (KWIN, QBLK), 0)
        qi = lax.broadcasted_iota(jnp.int32, (KWIN, QBLK), 1)
        for v in range(3):
            dist_i = jnp.abs(v * QBLK + qi - kj)
            valid = dist_i <= WINDOW
            dist = dist_i.astype(F32)
            for h in range(n_heads):
                cols = slice((h % grp) * QBLK, (h % grp + 1) * QBLK)
                bias_ref[v, h // grp, :, cols] = jnp.where(valid, -(slopes[h] * dist) * LOG2E, -jnp.inf)
        for h in range(n_heads):
            cols = slice((h % grp) * QBLK, (h % grp + 1) * QBLK)
            sinkrow_ref[h // grp, :, cols] = jnp.full((1, QBLK), sink_ref[h] * LOG2E, F32)

    def sub_block(j, carry):
        q0 = i * tq + j * QBLK
        k0 = pl.multiple_of(jnp.clip(q0 - WINDOW, 0, seq - KWIN), QBLK)
        variant = (q0 - k0) // QBLK
        rows = pl.ds(pl.multiple_of(j * QBLK, QBLK), QBLK)
        par = (i + j) % 2
        m_all = []
        for hk in range(n_kv):
            k = k_ref[0, pl.ds(k0, KWIN), hk * HEAD_DIM:(hk + 1) * HEAD_DIM]
            st = jnp.dot(k, qt_ref[j, hk], preferred_element_type=F32)
            st_ref[par, hk] = st + bias_ref[variant, hk]
            m_all.append(jnp.maximum(jnp.max(st_ref[par, hk], axis=0, keepdims=True), sinkrow_ref[hk]))
        ones = jnp.ones((SUM_ROWS, KWIN), BF16)
        for hk in range(n_kv):
            e = jnp.exp2(st_ref[par, hk] - m_all[hk]).astype(BF16)
            vt = jnp.concatenate([vt_ref[hk * HEAD_DIM:(hk + 1) * HEAD_DIM, pl.ds(k0, KWIN)], ones], axis=0)
            ot = jnp.dot(vt, e, preferred_element_type=F32)
            den = ot[HEAD_DIM:HEAD_DIM + 1, :] + jnp.exp2(sinkrow_ref[hk] - m_all[hk])
            ot = ot[:HEAD_DIM, :] * (1.0 / den)
            o4 = jnp.concatenate([ot[:, g * QBLK:(g + 1) * QBLK].T for g in range(grp)], axis=1)
            o_ref[0, rows, hk * grp * HEAD_DIM:(hk + 1) * grp * HEAD_DIM] = o4.astype(BF16)
        return carry

    for j in range(tq // QBLK):
        sub_block(j, 0)


def _attention(qt, k3, vt, sink, cast_a, cast_b, tq):
    b, s, kv_width = k3.shape
    n_kv = qt.shape[1]
    attn_width = n_kv * Q_PER_KV * HEAD_DIM
    steps = b * (s // tq)
    slab_a = pl.BlockSpec((cast_a.shape[0] // steps, cast_a.shape[1]), lambda bi, i: (bi * (s // tq) + i, 0))
    slab_b = pl.BlockSpec((cast_b.shape[0] // steps, cast_b.shape[1]), lambda bi, i: (bi * (s // tq) + i, 0))
    n_heads = attn_width // HEAD_DIM
    idx = np.arange(1, n_heads + 1, dtype=np.float32)
    slopes = tuple(float(v) for v in np.power(2.0, -8.0 * idx / n_heads).astype(np.float32))
    return pl.pallas_call(
        functools.partial(_attn_kernel, tq=tq, seq=s, slopes=slopes),
        grid=(b, s // tq),
        in_specs=[
            pl.BlockSpec(memory_space=pltpu.SMEM),
            pl.BlockSpec((tq // QBLK, n_kv, HEAD_DIM, Q_PER_KV * QBLK), lambda bi, i: (bi * (s // tq) + i, 0, 0, 0)),
            pl.BlockSpec((1, s, kv_width), lambda bi, i: (bi, 0, 0)),
            pl.BlockSpec((kv_width, s), lambda bi, i: (0, bi)),
            slab_a,
            slab_b,
        ],
        out_specs=[pl.BlockSpec((1, tq, attn_width), lambda bi, i: (bi, i, 0)), slab_a, slab_b],
        out_shape=[jax.ShapeDtypeStruct((b, s, attn_width), BF16),
                   jax.ShapeDtypeStruct(cast_a.shape, BF16),
                   jax.ShapeDtypeStruct(cast_b.shape, BF16)],
        scratch_shapes=[
            pltpu.VMEM((3, n_kv, KWIN, Q_PER_KV * QBLK), F32),
            pltpu.VMEM((n_kv, 1, Q_PER_KV * QBLK), F32),
            pltpu.VMEM((2, n_kv, KWIN, Q_PER_KV * QBLK), F32),
        ],
        compiler_params=pltpu.CompilerParams(
            dimension_semantics=("arbitrary", "arbitrary"), vmem_limit_bytes=V7X_VMEM_LIMIT),
        name="window_attn",
    )(sink, qt, k3, vt, cast_a, cast_b)


def _outproj_kernel(yp_ref, ya_ref, w_ref, x_ref, g_ref, o_ref, *, pool_width, tm):
    half = tm // 2
    for r0 in (0, half):
        rows = slice(r0, r0 + half)
        mix = jnp.dot(yp_ref[rows, :], w_ref[:pool_width, :], preferred_element_type=F32)
        mix = mix + jnp.dot(ya_ref[rows, :], w_ref[pool_width:, :], preferred_element_type=F32)
        o_ref[rows, :] = x_ref[rows, :] + _rms(mix, g_ref[...])


def _out_proj(yp, ya, w_out, x2, g, tm):
    n, d = x2.shape
    pw = yp.shape[1]
    aw = ya.shape[1]
    return pl.pallas_call(
        functools.partial(_outproj_kernel, pool_width=pw, tm=tm),
        grid=(n // tm,),
        in_specs=[
            pl.BlockSpec((tm, pw), lambda i: (i, 0)),
            pl.BlockSpec((tm, aw), lambda i: (i, 0)),
            pl.BlockSpec((pw + aw, d), lambda i: (0, 0), pipeline_mode=pl.Buffered(1)),
            pl.BlockSpec((tm, d), lambda i: (i, 0)),
            pl.BlockSpec((1, d), lambda i: (0, 0)),
        ],
        out_specs=pl.BlockSpec((tm, d), lambda i: (i, 0)),
        out_shape=jax.ShapeDtypeStruct((n, d), F32),
        compiler_params=pltpu.CompilerParams(
            dimension_semantics=("parallel",), vmem_limit_bytes=V7X_VMEM_LIMIT),
        name="out_proj",
    )(yp, ya, w_out, x2, g)


def _gelu_tanh(x):
    return 0.5 * x * (1.0 + jnp.tanh(np.sqrt(2.0 / np.pi).astype(np.float32) * (x + 0.044715 * (x * x * x))))


def _ffn_kernel(x_ref, xp_ref, xn_ref, gpre_ref, wg_ref, wu_ref, cw_ref, cb_ref, wd_ref, gpost_ref,
                o_ref, f_ref, ss_ref, *, tm, tiles_per_seq):
    i = pl.program_id(0)
    c = pl.program_id(1)
    hal = CONV_HALO

    @pl.when(c == 0)
    def _():
        it = i % tiles_per_seq
        gp = gpre_ref[...]
        f_ref[0:tm, :] = _rms(x_ref[...], gp).astype(BF16)
        edge = jnp.concatenate([_rms(xn_ref[...], gp), _rms(xp_ref[...], gp)], axis=0)
        row = lax.broadcasted_iota(jnp.int32, (hal, 1), 0)
        next_row = jnp.where(it < tiles_per_seq - 1, 0, -1)
        prev_row = jnp.where(it > 0, hal - 1, -1)
        f_ref[tm:, :] = jnp.where((row == next_row) | (row == prev_row), edge, 0.0).astype(BF16)
        o_ref[...] = jnp.zeros_like(o_ref)

    g = jnp.dot(f_ref[...], wg_ref[...], preferred_element_type=F32)
    up = jnp.dot(f_ref[0:tm, :], wu_ref[...], preferred_element_type=F32)
    cw = cw_ref[...]
    g_prev = pltpu.roll(g, 1, axis=0)[0:tm, :]
    g_next = pltpu.roll(g, tm + hal - 1, axis=0)[0:tm, :]
    gate = g_prev * cw[0:1, :] + g[0:tm, :] * cw[1:2, :] + g_next * cw[2:3, :] + cb_ref[...]
    hmid = (_gelu_tanh(gate) * up).astype(BF16)
    y = o_ref[...] + jnp.dot(hmid, wd_ref[...], preferred_element_type=F32)
    o_ref[...] = y
    ss_ref[...] = jnp.sum(y * y, axis=-1, keepdims=True)

    @pl.when(c == pl.num_programs(1) - 1)
    def _():
        rs = lax.rsqrt(ss_ref[...] * (1.0 / o_ref.shape[1]) + EPS)
        o_ref[...] = x_ref[...] + o_ref[...] * rs * gpost_ref[...]


def _ffn(x1, g_pre, w_gate, w_up, conv_w, conv_b, w_down, g_post, seq, tm, tc):
    n, d = x1.shape
    n_chunks = w_gate.shape[1] // tc
    hal = CONV_HALO
    nb8 = tm // 8
    tiles_per_seq = seq // tm
    return pl.pallas_call(
        functools.partial(_ffn_kernel, tm=tm, tiles_per_seq=tiles_per_seq),
        grid=(n // tm, n_chunks),
        in_specs=[
            pl.BlockSpec((tm, d), lambda i, c: (i, 0)),
            pl.BlockSpec((8, d), lambda i, c: (jnp.maximum(i * nb8 - 1, 0), 0)),
            pl.BlockSpec((8, d), lambda i, c: (jnp.minimum((i + 1) * nb8, n // 8 - 1), 0)),
            pl.BlockSpec((1, d), lambda i, c: (0, 0)),
            pl.BlockSpec((d, tc), lambda i, c: (0, c)),
            pl.BlockSpec((d, tc), lambda i, c: (0, c)),
            pl.BlockSpec((3, tc), lambda i, c: (0, c)),
            pl.BlockSpec((1, tc), lambda i, c: (0, c)),
            pl.BlockSpec((tc, d), lambda i, c: (c, 0)),
            pl.BlockSpec((1, d), lambda i, c: (0, 0)),
        ],
        out_specs=pl.BlockSpec((tm, d), lambda i, c: (i, 0)),
        out_shape=jax.ShapeDtypeStruct((n, d), F32),
        scratch_shapes=[
            pltpu.VMEM((tm + hal, d), BF16),
            pltpu.VMEM((tm, 1), F32),
        ],
        compiler_params=pltpu.CompilerParams(
            dimension_semantics=("parallel", "arbitrary"), vmem_limit_bytes=V7X_VMEM_LIMIT),
        name="conv_ffn",
    )(x1, x1, x1, g_pre, w_gate, w_up, conv_w, conv_b, w_down, g_post)


def kernel(x, norm_pre_mix, w_in, w_pool, pool_scale, attn_sink, w_out, norm_post_mix, norm_pre_ffn,
           w_gate, w_up, conv_w, conv_b, w_down, norm_post_ffn):
    b, s, d = x.shape
    depth = w_in.shape[0]
    pool_width = pool_scale.shape[1]
    n_heads = attn_sink.shape[1]
    attn_width = n_heads * HEAD_DIM
    x2 = x.reshape(b * s, d)
    for l in range(depth):
        y_pool, qt, k, vt, w_out_bf, w_gate_bf = _in_proj_pool(
            x2, norm_pre_mix[l][None, :], w_in[l].astype(BF16), w_pool[l].astype(BF16), pool_scale[l][None, :],
            w_out[l], w_gate[l], s, pool_width, attn_width, tm=1024)
        y_attn, w_up_bf, w_down_bf = _attention(qt, k.reshape(b, s, -1), vt, attn_sink[l], w_up[l], w_down[l],
                                                tq=1024)
        x2 = _out_proj(y_pool, y_attn.reshape(b * s, attn_width), w_out_bf, x2, norm_post_mix[l][None, :], tm=1024)
        x2 = _ffn(x2, norm_pre_ffn[l][None, :], w_gate_bf, w_up_bf, conv_w[l], conv_b[l][None, :], w_down_bf,
                  norm_post_ffn[l][None, :], seq=s, tm=1024, tc=FFN_CHUNK)
    return x2.reshape(b, s, d)
```

```python
import functools

import jax
import jax.numpy as jnp
import numpy as np
from jax import lax
from jax.experimental import pallas as pl
from jax.experimental.pallas import tpu as pltpu

F32 = jnp.float32
BF16 = jnp.bfloat16

EPS = 1e-6
LOG2E = 1.4426950408889634
POOL_WINDOWS = (2, 4, 8, 16)
HEAD_DIM = 64
Q_PER_KV = 4
WINDOW = 128
QBLK = 128
KWIN = 3 * QBLK
SUM_ROWS = 16
IN_HALO = 16
CONV_HALO = 16
FFN_CHUNK = 512
V7X_VMEM_LIMIT = 62 * 1024 * 1024


def _rms(x, g):
    return x * lax.rsqrt(jnp.mean(x * x, axis=-1, keepdims=True) + EPS) * g


def _roll_rows(a, shift):
    return pltpu.roll(a, shift % a.shape[0], axis=0)


def _inproj_pool_kernel(x_ref, xp_ref, xn_ref, g_ref, w_ref, wp_ref, ps_ref, ca_ref, cb_ref,
                        yp_ref, qt_ref, k_ref, vt_ref, ca_out_ref, cb_out_ref, h_ref,
                        *, tm, seq, pool_width, attn_width):
    ca_out_ref[...] = ca_ref[...].astype(BF16)
    cb_out_ref[...] = cb_ref[...].astype(BF16)
    i = pl.program_id(0)
    tiles_per_seq = seq // tm
    it = i % tiles_per_seq
    hal = IN_HALO
    half = tm // 2
    rows = tm + 2 * hal
    gw = pool_width // len(POOL_WINDOWS)
    gp = g_ref[...]

    h_ref[0:hal, :] = jnp.where(it > 0, _rms(xp_ref[...], gp), 0.0).astype(BF16)
    h_ref[hal:hal + half, :] = _rms(x_ref[0:half, :], gp).astype(BF16)
    h_ref[hal + half:hal + tm, :] = _rms(x_ref[half:tm, :], gp).astype(BF16)
    h_ref[hal + tm:, :] = jnp.where(it < tiles_per_seq - 1, _rms(xn_ref[...], gp), 0.0).astype(BF16)

    w_u = w_ref[:, :pool_width]
    u = jnp.concatenate([jnp.dot(h_ref[0:hal + half, :], w_u, preferred_element_type=F32),
                         jnp.dot(h_ref[hal + half:rows, :], w_u, preferred_element_type=F32)], axis=0)
    w_q = w_ref[:, pool_width:pool_width + attn_width]
    w_kv = w_ref[:, pool_width + attn_width:]
    kv_width = k_ref.shape[1]
    grp_width = Q_PER_KV * HEAD_DIM
    for r0 in (0, half):
        hm = h_ref[hal + r0:hal + r0 + half, :]
        q = jnp.dot(hm, w_q, preferred_element_type=F32) * (HEAD_DIM ** -0.5 * LOG2E)
        for jb in range(half // QBLK):
            for hk in range(attn_width // grp_width):
                t = q[jb * QBLK:(jb + 1) * QBLK, hk * grp_width:(hk + 1) * grp_width].T
                qt = jnp.concatenate([t[g * HEAD_DIM:(g + 1) * HEAD_DIM, :] for g in range(Q_PER_KV)], axis=1)
                qt_ref[r0 // QBLK + jb, hk] = qt.astype(BF16)
        kv = jnp.dot(hm, w_kv, preferred_element_type=F32)
        k_ref[r0:r0 + half, :] = kv[:, :kv_width].astype(BF16)
        vt_ref[:, r0:r0 + half] = kv[:, kv_width:].T.astype(BF16)

    a1 = u + _roll_rows(u, 1)
    a1r = a1[:, gw:]
    a2 = _roll_rows(a1r, -1) + _roll_rows(a1r, 1)
    a2r = a2[:, gw:]
    a4 = _roll_rows(a2r, -2) + _roll_rows(a2r, 2)
    a4r = a4[:, gw:]
    a8 = _roll_rows(a4r, -4) + _roll_rows(a4r, 4)
    main = slice(hal, hal + tm)
    wins = (a1[main, :gw], a2[main, :gw], a4[main, :gw], a8[main, :])
    t = it * tm + lax.broadcasted_iota(jnp.int32, (tm, 1), 0)
    for g, w in enumerate(POOL_WINDOWS):
        lo = jnp.maximum(t - w // 2, 0)
        hi = jnp.minimum(t + w // 2, seq)
        inv_cnt = 1.0 / (hi - lo).astype(F32)
        cols = slice(g * gw, (g + 1) * gw)
        d = (wins[g] * inv_cnt - u[main, cols]).astype(BF16)
        y = jnp.dot(d, wp_ref[g], preferred_element_type=F32) * ps_ref[:, cols]
        yp_ref[:, cols] = y.astype(BF16)


def _in_proj_pool(x2, g, w_in, w_pool, pool_scale, cast_a, cast_b, seq, pool_width, attn_width, tm):
    n, d = x2.shape
    in_width = w_in.shape[1]
    ng = len(POOL_WINDOWS)
    gw = pool_width // ng
    hal = IN_HALO
    nh = tm // hal
    steps = n // tm
    kv_width = (in_width - pool_width - attn_width) // 2
    n_kv = attn_width // (Q_PER_KV * HEAD_DIM)
    resident = dict(pipeline_mode=pl.Buffered(1))
    slab_a = pl.BlockSpec((cast_a.shape[0] // steps, cast_a.shape[1]), lambda i: (i, 0))
    slab_b = pl.BlockSpec((cast_b.shape[0] // steps, cast_b.shape[1]), lambda i: (i, 0))
    return pl.pallas_call(
        functools.partial(_inproj_pool_kernel, tm=tm, seq=seq, pool_width=pool_width, attn_width=attn_width),
        grid=(n // tm,),
        in_specs=[
            pl.BlockSpec((tm, d), lambda i: (i, 0)),
            pl.BlockSpec((hal, d), lambda i: (jnp.maximum(i * nh - 1, 0), 0)),
            pl.BlockSpec((hal, d), lambda i: (jnp.minimum((i + 1) * nh, n // hal - 1), 0)),
            pl.BlockSpec((1, d), lambda i: (0, 0)),
            pl.BlockSpec((d, in_width), lambda i: (0, 0), **resident),
            pl.BlockSpec((ng, gw, gw), lambda i: (0, 0, 0), **resident),
            pl.BlockSpec((1, pool_width), lambda i: (0, 0)),
            slab_a,
            slab_b,
        ],
        out_specs=[
            pl.BlockSpec((tm, pool_width), lambda i: (i, 0)),
            pl.BlockSpec((tm // QBLK, n_kv, HEAD_DIM, Q_PER_KV * QBLK), lambda i: (i, 0, 0, 0)),
            pl.BlockSpec((tm, kv_width), lambda i: (i, 0)),
            pl.BlockSpec((kv_width, tm), lambda i: (0, i)),
            slab_a,
            slab_b,
        ],
        out_shape=[
            jax.ShapeDtypeStruct((n, pool_width), BF16),
            jax.ShapeDtypeStruct((n // QBLK, n_kv, HEAD_DIM, Q_PER_KV * QBLK), BF16),
            jax.ShapeDtypeStruct((n, kv_width), BF16),
            jax.ShapeDtypeStruct((kv_width, n), BF16),
            jax.ShapeDtypeStruct(cast_a.shape, BF16),
            jax.ShapeDtypeStruct(cast_b.shape, BF16),
        ],
        scratch_shapes=[pltpu.VMEM((tm + 2 * hal, d), BF16)],
        compiler_params=pltpu.CompilerParams(
            dimension_semantics=("parallel",), vmem_limit_bytes=V7X_VMEM_LIMIT),
        name="in_proj_pool",
    )(x2, x2, x2, g, w_in, w_pool, pool_scale, cast_a, cast_b)


def _attn_kernel(sink_ref, qt_ref, k_ref, vt_ref, ca_ref, cb_ref, o_ref, ca_out_ref, cb_out_ref,
                 bias_ref, sinkrow_ref, st_ref, *, tq, seq, slopes):
    ca_out_ref[...] = ca_ref[...].astype(BF16)
    cb_out_ref[...] = cb_ref[...].astype(BF16)
    bi = pl.program_id(0)
    i = pl.program_id(1)
    grp = Q_PER_KV
    n_heads = len(slopes)
    n_kv = n_heads // grp

    @pl.when((bi == 0) & (i == 0))
    def _():
        kj = lax.broadcasted_iota(jnp.int32, (KWIN, QBLK), 0)
        qi = lax.broadcasted_iota(jnp.int32, (KWIN, QBLK), 1)
        for v in range(3):
            dist_i = jnp.abs(v * QBLK + qi - kj)
            valid = dist_i <= WINDOW
            dist = dist_i.astype(F32)
            for h in range(n_heads):
                cols = slice((h % grp) * QBLK, (h % grp + 1) * QBLK)
                bias_ref[v, h // grp, :, cols] = jnp.where(valid, -(slopes[h] * dist) * LOG2E, -jnp.inf)
        for h in range(n_heads):
            cols = slice((h % grp) * QBLK, (h % grp + 1) * QBLK)
            sinkrow_ref[h // grp, :, cols] = jnp.full((1, QBLK), sink_ref[h] * LOG2E, F32)

    def sub_block(j, carry):
        q0 = i * tq + j * QBLK
        k0 = pl.multiple_of(jnp.clip(q0 - WINDOW, 0, seq - KWIN), QBLK)
        variant = (q0 - k0) // QBLK
        rows = pl.ds(pl.multiple_of(j * QBLK, QBLK), QBLK)
        par = (i + j) % 2
        m_all = []
        for hk in range(n_kv):
            k = k_ref[0, pl.ds(k0, KWIN), hk * HEAD_DIM:(hk + 1) * HEAD_DIM]
            st = jnp.dot(k, qt_ref[j, hk], preferred_element_type=F32)
            st_ref[par, hk] = st + bias_ref[variant, hk]
            m_all.append(jnp.maximum(jnp.max(st_ref[par, hk], axis=0, keepdims=True), sinkrow_ref[hk]))
        ones = jnp.ones((SUM_ROWS, KWIN), BF16)
        for hk in range(n_kv):
            e = jnp.exp2(st_ref[par, hk] - m_all[hk]).astype(BF16)
            vt = jnp.concatenate([vt_ref[hk * HEAD_DIM:(hk + 1) * HEAD_DIM, pl.ds(k0, KWIN)], ones], axis=0)
            ot = jnp.dot(vt, e, preferred_element_type=F32)
            den = ot[HEAD_DIM:HEAD_DIM + 1, :] + jnp.exp2(sinkrow_ref[hk] - m_all[hk])
            ot = ot[:HEAD_DIM, :] * (1.0 / den)
            o4 = jnp.concatenate([ot[:, g * QBLK:(g + 1) * QBLK].T for g in range(grp)], axis=1)
            o_ref[0, rows, hk * grp * HEAD_DIM:(hk + 1) * grp * HEAD_DIM] = o4.astype(BF16)
        return carry

    for j in range(tq // QBLK):
        sub_block(j, 0)


def _attention(qt, k3, vt, sink, cast_a, cast_b, tq):
    b, s, kv_width = k3.shape
    n_kv = qt.shape[1]
    attn_width = n_kv * Q_PER_KV * HEAD_DIM
    steps = b * (s // tq)
    slab_a = pl.BlockSpec((cast_a.shape[0] // steps, cast_a.shape[1]), lambda bi, i: (bi * (s // tq) + i, 0))
    slab_b = pl.BlockSpec((cast_b.shape[0] // steps, cast_b.shape[1]), lambda bi, i: (bi * (s // tq) + i, 0))
    n_heads = attn_width // HEAD_DIM
    idx = np.arange(1, n_heads + 1, dtype=np.float32)
    slopes = tuple(float(v) for v in np.power(2.0, -8.0 * idx / n_heads).astype(np.float32))
    return pl.pallas_call(
        functools.partial(_attn_kernel, tq=tq, seq=s, slopes=slopes),
        grid=(b, s // tq),
        in_specs=[
            pl.BlockSpec(memory_space=pltpu.SMEM),
            pl.BlockSpec((tq // QBLK, n_kv, HEAD_DIM, Q_PER_KV * QBLK), lambda bi, i: (bi * (s // tq) + i, 0, 0, 0)),
            pl.BlockSpec((1, s, kv_width), lambda bi, i: (bi, 0, 0)),
            pl.BlockSpec((kv_width, s), lambda bi, i: (0, bi)),
            slab_a,
            slab_b,
        ],
        out_specs=[pl.BlockSpec((1, tq, attn_width), lambda bi, i: (bi, i, 0)), slab_a, slab_b],
        out_shape=[jax.ShapeDtypeStruct((b, s, attn_width), BF16),
                   jax.ShapeDtypeStruct(cast_a.shape, BF16),
                   jax.ShapeDtypeStruct(cast_b.shape, BF16)],
        scratch_shapes=[
            pltpu.VMEM((3, n_kv, KWIN, Q_PER_KV * QBLK), F32),
            pltpu.VMEM((n_kv, 1, Q_PER_KV * QBLK), F32),
            pltpu.VMEM((2, n_kv, KWIN, Q_PER_KV * QBLK), F32),
        ],
        compiler_params=pltpu.CompilerParams(
            dimension_semantics=("arbitrary", "arbitrary"), vmem_limit_bytes=V7X_VMEM_LIMIT),
        name="window_attn",
    )(sink, qt, k3, vt, cast_a, cast_b)


def _outproj_kernel(yp_ref, ya_ref, w_ref, x_ref, g_ref, o_ref, *, pool_width, tm):
    half = tm // 2
    for r0 in (0, half):
        rows = slice(r0, r0 + half)
        mix = jnp.dot(yp_ref[rows, :], w_ref[:pool_width, :], preferred_element_type=F32)
        mix = mix + jnp.dot(ya_ref[rows, :], w_ref[pool_width:, :], preferred_element_type=F32)
        o_ref[rows, :] = x_ref[rows, :] + _rms(mix, g_ref[...])


def _out_proj(yp, ya, w_out, x2, g, tm):
    n, d = x2.shape
    pw = yp.shape[1]
    aw = ya.shape[1]
    return pl.pallas_call(
        functools.partial(_outproj_kernel, pool_width=pw, tm=tm),
        grid=(n // tm,),
        in_specs=[
            pl.BlockSpec((tm, pw), lambda i: (i, 0)),
            pl.BlockSpec((tm, aw), lambda i: (i, 0)),
            pl.BlockSpec((pw + aw, d), lambda i: (0, 0), pipeline_mode=pl.Buffered(1)),
            pl.BlockSpec((tm, d), lambda i: (i, 0)),
            pl.BlockSpec((1, d), lambda i: (0, 0)),
        ],
        out_specs=pl.BlockSpec((tm, d), lambda i: (i, 0)),
        out_shape=jax.ShapeDtypeStruct((n, d), F32),
        compiler_params=pltpu.CompilerParams(
            dimension_semantics=("parallel",), vmem_limit_bytes=V7X_VMEM_LIMIT),
        name="out_proj",
    )(yp, ya, w_out, x2, g)


def _gelu_tanh(x):
    return 0.5 * x * (1.0 + jnp.tanh(np.sqrt(2.0 / np.pi).astype(np.float32) * (x + 0.044715 * (x * x * x))))


def _ffn_kernel(x_ref, xp_ref, xn_ref, gpre_ref, wg_ref, wu_ref, cw_ref, cb_ref, wd_ref, gpost_ref,
                o_ref, f_ref, ss_ref, *, tm, tiles_per_seq):
    i = pl.program_id(0)
    c = pl.program_id(1)
    hal = CONV_HALO

    @pl.when(c == 0)
    def _():
        it = i % tiles_per_seq
        gp = gpre_ref[...]
        f_ref[0:tm, :] = _rms(x_ref[...], gp).astype(BF16)
        edge = jnp.concatenate([_rms(xn_ref[...], gp), _rms(xp_ref[...], gp)], axis=0)
        row = lax.broadcasted_iota(jnp.int32, (hal, 1), 0)
        next_row = jnp.where(it < tiles_per_seq - 1, 0, -1)
        prev_row = jnp.where(it > 0, hal - 1, -1)
        f_ref[tm:, :] = jnp.where((row == next_row) | (row == prev_row), edge, 0.0).astype(BF16)
        o_ref[...] = jnp.zeros_like(o_ref)

    g = jnp.dot(f_ref[...], wg_ref[...], preferred_element_type=F32)
    up = jnp.dot(f_ref[0:tm, :], wu_ref[...], preferred_element_type=F32)
    cw = cw_ref[...]
    g_prev = pltpu.roll(g, 1, axis=0)[0:tm, :]
    g_next = pltpu.roll(g, tm + hal - 1, axis=0)[0:tm, :]
    gate = g_prev * cw[0:1, :] + g[0:tm, :] * cw[1:2, :] + g_next * cw[2:3, :] + cb_ref[...]
    hmid = (_gelu_tanh(gate) * up).astype(BF16)
    y = o_ref[...] + jnp.dot(hmid, wd_ref[...], preferred_element_type=F32)
    o_ref[...] = y
    ss_ref[...] = jnp.sum(y * y, axis=-1, keepdims=True)

    @pl.when(c == pl.num_programs(1) - 1)
    def _():
        rs = lax.rsqrt(ss_ref[...] * (1.0 / o_ref.shape[1]) + EPS)
        o_ref[...] = x_ref[...] + o_ref[...] * rs * gpost_ref[...]


def _ffn(x1, g_pre, w_gate, w_up, conv_w, conv_b, w_down, g_post, seq, tm, tc):
    n, d = x1.shape
    n_chunks = w_gate.shape[1] // tc
    hal = CONV_HALO
    nb8 = tm // 8
    tiles_per_seq = seq // tm
    return pl.pallas_call(
        functools.partial(_ffn_kernel, tm=tm, tiles_per_seq=tiles_per_seq),
        grid=(n // tm, n_chunks),
        in_specs=[
            pl.BlockSpec((tm, d), lambda i, c: (i, 0)),
            pl.BlockSpec((8, d), lambda i, c: (jnp.maximum(i * nb8 - 1, 0), 0)),
            pl.BlockSpec((8, d), lambda i, c: (jnp.minimum((i + 1) * nb8, n // 8 - 1), 0)),
            pl.BlockSpec((1, d), lambda i, c: (0, 0)),
            pl.BlockSpec((d, tc), lambda i, c: (0, c)),
            pl.BlockSpec((d, tc), lambda i, c: (0, c)),
            pl.BlockSpec((3, tc), lambda i, c: (0, c)),
            pl.BlockSpec((1, tc), lambda i, c: (0, c)),
            pl.BlockSpec((tc, d), lambda i, c: (c, 0)),
            pl.BlockSpec((1, d), lambda i, c: (0, 0)),
        ],
        out_specs=pl.BlockSpec((tm, d), lambda i, c: (i, 0)),
        out_shape=jax.ShapeDtypeStruct((n, d), F32),
        scratch_shapes=[
            pltpu.VMEM((tm + hal, d), BF16),
            pltpu.VMEM((tm, 1), F32),
        ],
        compiler_params=pltpu.CompilerParams(
            dimension_semantics=("parallel", "arbitrary"), vmem_limit_bytes=V7X_VMEM_LIMIT),
        name="conv_ffn",
    )(x1, x1, x1, g_pre, w_gate, w_up, conv_w, conv_b, w_down, g_post)


def _gate_up_kernel(x_ref, xp_ref, xn_ref, gpre_ref, wg_ref, wu_ref, cw_ref, cb_ref, h_ref, f_ref,
                    *, tm, tiles_per_seq):
    i = pl.program_id(0)
    c = pl.program_id(1)
    hal = CONV_HALO

    @pl.when(c == 0)
    def _():
        it = i % tiles_per_seq
        gp = gpre_ref[...]
        f_ref[0:tm, :] = _rms(x_ref[...], gp).astype(BF16)
        edge = jnp.concatenate([_rms(xn_ref[...], gp), _rms(xp_ref[...], gp)], axis=0)
        row = lax.broadcasted_iota(jnp.int32, (hal, 1), 0)
        next_row = jnp.where(it < tiles_per_seq - 1, 0, -1)
        prev_row = jnp.where(it > 0, hal - 1, -1)
        f_ref[tm:, :] = jnp.where((row == next_row) | (row == prev_row), edge, 0.0).astype(BF16)

    g = jnp.dot(f_ref[...], wg_ref[...], preferred_element_type=F32)
    up = jnp.dot(f_ref[0:tm, :], wu_ref[...], preferred_element_type=F32)
    cw = cw_ref[...]
    g_prev = pltpu.roll(g, 1, axis=0)[0:tm, :]
    g_next = pltpu.roll(g, tm + hal - 1, axis=0)[0:tm, :]
    gate = g_prev * cw[0:1, :] + g[0:tm, :] * cw[1:2, :] + g_next * cw[2:3, :] + cb_ref[...]
    h_ref[...] = (_gelu_tanh(gate) * up).astype(BF16)


def _gate_up(x1, g_pre, w_gate, w_up, conv_w, conv_b, seq, tm, tc):
    n, d = x1.shape
    dff = w_gate.shape[1]
    hal = CONV_HALO
    nb8 = tm // 8
    tiles_per_seq = seq // tm
    return pl.pallas_call(
        functools.partial(_gate_up_kernel, tm=tm, tiles_per_seq=tiles_per_seq),
        grid=(n // tm, dff // tc),
        in_specs=[
            pl.BlockSpec((tm, d), lambda i, c: (i, 0)),
            pl.BlockSpec((8, d), lambda i, c: (jnp.maximum(i * nb8 - 1, 0), 0)),
            pl.BlockSpec((8, d), lambda i, c: (jnp.minimum((i + 1) * nb8, n // 8 - 1), 0)),
            pl.BlockSpec((1, d), lambda i, c: (0, 0)),
            pl.BlockSpec((d, tc), lambda i, c: (0, c)),
            pl.BlockSpec((d, tc), lambda i, c: (0, c)),
            pl.BlockSpec((3, tc), lambda i, c: (0, c)),
            pl.BlockSpec((1, tc), lambda i, c: (0, c)),
        ],
        out_specs=pl.BlockSpec((tm, tc), lambda i, c: (i, c)),
        out_shape=jax.ShapeDtypeStruct((n, dff), BF16),
        scratch_shapes=[pltpu.VMEM((tm + hal, d), BF16)],
        compiler_params=pltpu.CompilerParams(
            dimension_semantics=("parallel", "arbitrary"), vmem_limit_bytes=V7X_VMEM_LIMIT),
        name="ffn_gate_up",
    )(x1, x1, x1, g_pre, w_gate, w_up, conv_w, conv_b)


def _down_kernel(h_ref, w_ref, x_ref, g_ref, o_ref, *, tm):
    half = tm // 2
    for r0 in (0, half):
        rows = slice(r0, r0 + half)
        y = jnp.dot(h_ref[rows, :], w_ref[...], preferred_element_type=F32)
        o_ref[rows, :] = x_ref[rows, :] + _rms(y, g_ref[...])


def _down(h, w_down, x1, g_post, tm):
    n, d = x1.shape
    dff = h.shape[1]
    return pl.pallas_call(
        functools.partial(_down_kernel, tm=tm),
        grid=(n // tm,),
        in_specs=[
            pl.BlockSpec((tm, dff), lambda i: (i, 0)),
            pl.BlockSpec((dff, d), lambda i: (0, 0), pipeline_mode=pl.Buffered(1)),
            pl.BlockSpec((tm, d), lambda i: (i, 0)),
            pl.BlockSpec((1, d), lambda i: (0, 0)),
        ],
        out_specs=pl.BlockSpec((tm, d), lambda i: (i, 0)),
        out_shape=jax.ShapeDtypeStruct((n, d), F32),
        compiler_params=pltpu.CompilerParams(
            dimension_semantics=("parallel",), vmem_limit_bytes=V7X_VMEM_LIMIT),
        name="ffn_down",
    )(h, w_down, x1, g_post)


def kernel(x, norm_pre_mix, w_in, w_pool, pool_scale, attn_sink, w_out, norm_post_mix, norm_pre_ffn,
           w_gate, w_up, conv_w, conv_b, w_down, norm_post_ffn):
    b, s, d = x.shape
    depth = w_in.shape[0]
    pool_width = pool_scale.shape[1]
    n_heads = attn_sink.shape[1]
    attn_width = n_heads * HEAD_DIM
    x2 = x.reshape(b * s, d)
    for l in range(depth):
        y_pool, qt, k, vt, w_out_bf, w_gate_bf = _in_proj_pool(
            x2, norm_pre_mix[l][None, :], w_in[l].astype(BF16), w_pool[l].astype(BF16), pool_scale[l][None, :],
            w_out[l], w_gate[l], s, pool_width, attn_width, tm=1024)
        y_attn, w_up_bf, w_down_bf = _attention(qt, k.reshape(b, s, -1), vt, attn_sink[l], w_up[l], w_down[l],
                                                tq=1024)
        x2 = _out_proj(y_pool, y_attn.reshape(b * s, attn_width), w_out_bf, x2, norm_post_mix[l][None, :], tm=1024)
        hmid = _gate_up(x2, norm_pre_ffn[l][None, :], w_gate_bf, w_up_bf, conv_w[l], conv_b[l][None, :], seq=s,
                        tm=1024, tc=FFN_CHUNK)
        x2 = _down(hmid, w_down_bf, x2, norm_post_ffn[l][None, :], tm=512)
    return x2.reshape(b, s, d)
```

```python
import functools

import jax
import jax.numpy as jnp
import numpy as np
from jax import lax
from jax.experimental import pallas as pl
from jax.experimental.pallas import tpu as pltpu

F32 = jnp.float32
BF16 = jnp.bfloat16

EPS = 1e-6
LOG2E = 1.4426950408889634
POOL_WINDOWS = (2, 4, 8, 16)
HEAD_DIM = 64
Q_PER_KV = 4
WINDOW = 128
QBLK = 128
KWIN = 3 * QBLK
SUM_ROWS = 16
IN_HALO = 16
CONV_HALO = 16
FFN_CHUNK = 512
ROW_TILE = 1024
DOWN_TILE = 512
V7X_VMEM_LIMIT = 62 * 1024 * 1024


def _rms(x, g):
    return x * lax.rsqrt(jnp.mean(x * x, axis=-1, keepdims=True) + EPS) * g


def _roll_rows(a, shift):
    return pltpu.roll(a, shift % a.shape[0], axis=0)


def _inproj_pool_kernel(x_ref, xp_ref, xn_ref, g_ref, w_ref, wp_ref, ps_ref, ca_ref, cb_ref,
                        yp_ref, qt_ref, k_ref, vt_ref, ca_out_ref, cb_out_ref, h_ref,
                        *, tm, seq, pool_width, attn_width):
    ca_out_ref[...] = ca_ref[...].astype(BF16)
    cb_out_ref[...] = cb_ref[...].astype(BF16)
    i = pl.program_id(0)
    tiles_per_seq = seq // tm
    it = i % tiles_per_seq
    hal = IN_HALO
    half = tm // 2
    rows = tm + 2 * hal
    gw = pool_width // len(POOL_WINDOWS)
    gp = g_ref[...]

    h_ref[0:hal, :] = jnp.where(it > 0, _rms(xp_ref[...], gp), 0.0).astype(BF16)
    h_ref[hal:hal + half, :] = _rms(x_ref[0:half, :], gp).astype(BF16)
    h_ref[hal + half:hal + tm, :] = _rms(x_ref[half:tm, :], gp).astype(BF16)
    h_ref[hal + tm:, :] = jnp.where(it < tiles_per_seq - 1, _rms(xn_ref[...], gp), 0.0).astype(BF16)

    w_u = w_ref[:, :pool_width]
    u = jnp.concatenate([jnp.dot(h_ref[0:hal + half, :], w_u, preferred_element_type=F32),
                         jnp.dot(h_ref[hal + half:rows, :], w_u, preferred_element_type=F32)], axis=0)
    w_q = w_ref[:, pool_width:pool_width + attn_width]
    w_kv = w_ref[:, pool_width + attn_width:]
    kv_width = k_ref.shape[1]
    grp_width = Q_PER_KV * HEAD_DIM
    for r0 in (0, half):
        hm = h_ref[hal + r0:hal + r0 + half, :]
        q = jnp.dot(hm, w_q, preferred_element_type=F32) * (HEAD_DIM ** -0.5 * LOG2E)
        for jb in range(half // QBLK):
            for hk in range(attn_width // grp_width):
                t = q[jb * QBLK:(jb + 1) * QBLK, hk * grp_width:(hk + 1) * grp_width].T
                qt = jnp.concatenate([t[g * HEAD_DIM:(g + 1) * HEAD_DIM, :] for g in range(Q_PER_KV)], axis=1)
                qt_ref[r0 // QBLK + jb, hk] = qt.astype(BF16)
        kv = jnp.dot(hm, w_kv, preferred_element_type=F32)
        k_ref[r0:r0 + half, :] = kv[:, :kv_width].astype(BF16)
        vt_ref[:, r0:r0 + half] = kv[:, kv_width:].T.astype(BF16)

    a1 = u + _roll_rows(u, 1)
    a1r = a1[:, gw:]
    a2 = _roll_rows(a1r, -1) + _roll_rows(a1r, 1)
    a2r = a2[:, gw:]
    a4 = _roll_rows(a2r, -2) + _roll_rows(a2r, 2)
    a4r = a4[:, gw:]
    a8 = _roll_rows(a4r, -4) + _roll_rows(a4r, 4)
    main = slice(hal, hal + tm)
    wins = (a1[main, :gw], a2[main, :gw], a4[main, :gw], a8[main, :])
    t = it * tm + lax.broadcasted_iota(jnp.int32, (tm, 1), 0)
    for g, w in enumerate(POOL_WINDOWS):
        lo = jnp.maximum(t - w // 2, 0)
        hi = jnp.minimum(t + w // 2, seq)
        inv_cnt = 1.0 / (hi - lo).astype(F32)
        cols = slice(g * gw, (g + 1) * gw)
        d = (wins[g] * inv_cnt - u[main, cols]).astype(BF16)
        y = jnp.dot(d, wp_ref[g], preferred_element_type=F32) * ps_ref[:, cols]
        yp_ref[:, cols] = y.astype(BF16)


def _in_proj_pool(x2, g, w_in, w_pool, pool_scale, cast_a, cast_b, seq, pool_width, attn_width, tm):
    n, d = x2.shape
    in_width = w_in.shape[1]
    ng = len(POOL_WINDOWS)
    gw = pool_width // ng
    hal = IN_HALO
    nh = tm // hal
    steps = n // tm
    kv_width = (in_width - pool_width - attn_width) // 2
    n_kv = attn_width // (Q_PER_KV * HEAD_DIM)
    resident = dict(pipeline_mode=pl.Buffered(1))
    slab_a = pl.BlockSpec((cast_a.shape[0] // steps, cast_a.shape[1]), lambda i: (i, 0))
    slab_b = pl.BlockSpec((cast_b.shape[0] // steps, cast_b.shape[1]), lambda i: (i, 0))
    return pl.pallas_call(
        functools.partial(_inproj_pool_kernel, tm=tm, seq=seq, pool_width=pool_width, attn_width=attn_width),
        grid=(n // tm,),
        in_specs=[
            pl.BlockSpec((tm, d), lambda i: (i, 0)),
            pl.BlockSpec((hal, d), lambda i: (jnp.maximum(i * nh - 1, 0), 0)),
            pl.BlockSpec((hal, d), lambda i: (jnp.minimum((i + 1) * nh, n // hal - 1), 0)),
            pl.BlockSpec((1, d), lambda i: (0, 0)),
            pl.BlockSpec((d, in_width), lambda i: (0, 0), **resident),
            pl.BlockSpec((ng, gw, gw), lambda i: (0, 0, 0), **resident),
            pl.BlockSpec((1, pool_width), lambda i: (0, 0)),
            slab_a,
            slab_b,
        ],
        out_specs=[
            pl.BlockSpec((tm, pool_width), lambda i: (i, 0)),
            pl.BlockSpec((tm // QBLK, n_kv, HEAD_DIM, Q_PER_KV * QBLK), lambda i: (i, 0, 0, 0)),
            pl.BlockSpec((tm, kv_width), lambda i: (i, 0)),
            pl.BlockSpec((kv_width, tm), lambda i: (0, i)),
            slab_a,
            slab_b,
        ],
        out_shape=[
            jax.ShapeDtypeStruct((n, pool_width), BF16),
            jax.ShapeDtypeStruct((n // QBLK, n_kv, HEAD_DIM, Q_PER_KV * QBLK), BF16),
            jax.ShapeDtypeStruct((n, kv_width), BF16),
            jax.ShapeDtypeStruct((kv_width, n), BF16),
            jax.ShapeDtypeStruct(cast_a.shape, BF16),
            jax.ShapeDtypeStruct(cast_b.shape, BF16),
        ],
        scratch_shapes=[pltpu.VMEM((tm + 2 * hal, d), BF16)],
        compiler_params=pltpu.CompilerParams(
            dimension_semantics=("parallel",), vmem_limit_bytes=V7X_VMEM_LIMIT),
        name="in_proj_pool",
    )(x2, x2, x2, g, w_in, w_pool, pool_scale, cast_a, cast_b)


def _attn_kernel(sink_ref, qt_ref, k_ref, vt_ref, ca_ref, cb_ref, o_ref, ca_out_ref, cb_out_ref,
                 bias_ref, sinkrow_ref, st_ref, *, tq, seq, slopes):
    ca_out_ref[...] = ca_ref[...].astype(BF16)
    cb_out_ref[...] = cb_ref[...].astype(BF16)
    bi = pl.program_id(0)
    i = pl.program_id(1)
    grp = Q_PER_KV
    n_heads = len(slopes)
    n_kv = n_heads // grp

    @pl.when((bi == 0) & (i == 0))
    def _():
        kj = lax.broadcasted_iota(jnp.int32, (KWIN, QBLK), 0)
        qi = lax.broadcasted_iota(jnp.int32, (KWIN, QBLK), 1)
        for v in range(3):
            dist_i = jnp.abs(v * QBLK + qi - kj)
            valid = dist_i <= WINDOW
            dist = dist_i.astype(F32)
            for h in range(n_heads):
                cols = slice((h % grp) * QBLK, (h % grp + 1) * QBLK)
                bias_ref[v, h // grp, :, cols] = jnp.where(valid, -(slopes[h] * dist) * LOG2E, -jnp.inf)
        for h in range(n_heads):
            cols = slice((h % grp) * QBLK, (h % grp + 1) * QBLK)
            sinkrow_ref[h // grp, :, cols] = jnp.full((1, QBLK), sink_ref[h] * LOG2E, F32)

    def sub_block(j, carry):
        q0 = i * tq + j * QBLK
        k0 = pl.multiple_of(jnp.clip(q0 - WINDOW, 0, seq - KWIN), QBLK)
        variant = (q0 - k0) // QBLK
        rows = pl.ds(pl.multiple_of(j * QBLK, QBLK), QBLK)
        par = (i + j) % 2
        m_all = []
        for hk in range(n_kv):
            k = k_ref[0, pl.ds(k0, KWIN), hk * HEAD_DIM:(hk + 1) * HEAD_DIM]
            st = jnp.dot(k, qt_ref[j, hk], preferred_element_type=F32)
            st_ref[par, hk] = st + bias_ref[variant, hk]
            m_all.append(jnp.maximum(jnp.max(st_ref[par, hk], axis=0, keepdims=True), sinkrow_ref[hk]))
        ones = jnp.ones((SUM_ROWS, KWIN), BF16)
        for hk in range(n_kv):
            e = jnp.exp2(st_ref[par, hk] - m_all[hk]).astype(BF16)
            vt = jnp.concatenate([vt_ref[hk * HEAD_DIM:(hk + 1) * HEAD_DIM, pl.ds(k0, KWIN)], ones], axis=0)
            ot = jnp.dot(vt, e, preferred_element_type=F32)
            den = ot[HEAD_DIM:HEAD_DIM + 1, :] + jnp.exp2(sinkrow_ref[hk] - m_all[hk])
            ot = ot[:HEAD_DIM, :] * (1.0 / den)
            o4 = jnp.concatenate([ot[:, g * QBLK:(g + 1) * QBLK].T for g in range(grp)], axis=1)
            o_ref[0, rows, hk * grp * HEAD_DIM:(hk + 1) * grp * HEAD_DIM] = o4.astype(BF16)
        return carry

    for j in range(tq // QBLK):
        sub_block(j, 0)


def _attention(qt, k3, vt, sink, cast_a, cast_b, tq):
    b, s, kv_width = k3.shape
    n_kv = qt.shape[1]
    attn_width = n_kv * Q_PER_KV * HEAD_DIM
    steps = b * (s // tq)
    slab_a = pl.BlockSpec((cast_a.shape[0] // steps, cast_a.shape[1]), lambda bi, i: (bi * (s // tq) + i, 0))
    slab_b = pl.BlockSpec((cast_b.shape[0] // steps, cast_b.shape[1]), lambda bi, i: (bi * (s // tq) + i, 0))
    n_heads = attn_width // HEAD_DIM
    idx = np.arange(1, n_heads + 1, dtype=np.float32)
    slopes = tuple(float(v) for v in np.power(2.0, -8.0 * idx / n_heads).astype(np.float32))
    return pl.pallas_call(
        functools.partial(_attn_kernel, tq=tq, seq=s, slopes=slopes),
        grid=(b, s // tq),
        in_specs=[
            pl.BlockSpec(memory_space=pltpu.SMEM),
            pl.BlockSpec((tq // QBLK, n_kv, HEAD_DIM, Q_PER_KV * QBLK), lambda bi, i: (bi * (s // tq) + i, 0, 0, 0)),
            pl.BlockSpec((1, s, kv_width), lambda bi, i: (bi, 0, 0)),
            pl.BlockSpec((kv_width, s), lambda bi, i: (0, bi)),
            slab_a,
            slab_b,
        ],
        out_specs=[pl.BlockSpec((1, tq, attn_width), lambda bi, i: (bi, i, 0)), slab_a, slab_b],
        out_shape=[jax.ShapeDtypeStruct((b, s, attn_width), BF16),
                   jax.ShapeDtypeStruct(cast_a.shape, BF16),
                   jax.ShapeDtypeStruct(cast_b.shape, BF16)],
        scratch_shapes=[
            pltpu.VMEM((3, n_kv, KWIN, Q_PER_KV * QBLK), F32),
            pltpu.VMEM((n_kv, 1, Q_PER_KV * QBLK), F32),
            pltpu.VMEM((2, n_kv, KWIN, Q_PER_KV * QBLK), F32),
        ],
        compiler_params=pltpu.CompilerParams(
            dimension_semantics=("arbitrary", "arbitrary"), vmem_limit_bytes=V7X_VMEM_LIMIT),
        name="window_attn",
    )(sink, qt, k3, vt, cast_a, cast_b)


def _outproj_kernel(yp_ref, ya_ref, w_ref, x_ref, g_ref, o_ref, *, pool_width, tm):
    half = tm // 2
    for r0 in (0, half):
        rows = slice(r0, r0 + half)
        mix = jnp.dot(yp_ref[rows, :], w_ref[:pool_width, :], preferred_element_type=F32)
        mix = mix + jnp.dot(ya_ref[rows, :], w_ref[pool_width:, :], preferred_element_type=F32)
        o_ref[rows, :] = x_ref[rows, :] + _rms(mix, g_ref[...])


def _out_proj(yp, ya, w_out, x2, g, tm):
    n, d = x2.shape
    pw = yp.shape[1]
    aw = ya.shape[1]
    return pl.pallas_call(
        functools.partial(_outproj_kernel, pool_width=pw, tm=tm),
        grid=(n // tm,),
        in_specs=[
            pl.BlockSpec((tm, pw), lambda i: (i, 0)),
            pl.BlockSpec((tm, aw), lambda i: (i, 0)),
            pl.BlockSpec((pw + aw, d), lambda i: (0, 0), pipeline_mode=pl.Buffered(1)),
            pl.BlockSpec((tm, d), lambda i: (i, 0)),
            pl.BlockSpec((1, d), lambda i: (0, 0)),
        ],
        out_specs=pl.BlockSpec((tm, d), lambda i: (i, 0)),
        out_shape=jax.ShapeDtypeStruct((n, d), F32),
        compiler_params=pltpu.CompilerParams(
            dimension_semantics=("parallel",), vmem_limit_bytes=V7X_VMEM_LIMIT),
        name="out_proj",
    )(yp, ya, w_out, x2, g)


def _gelu_tanh(x):
    return 0.5 * x * (1.0 + jnp.tanh(np.sqrt(2.0 / np.pi).astype(np.float32) * (x + 0.044715 * (x * x * x))))


def _gate_up_kernel(x_ref, xp_ref, xn_ref, gpre_ref, wg_ref, wu_ref, cw_ref, cb_ref, h_ref, f_ref,
                    *, tm, tiles_per_seq):
    i = pl.program_id(0)
    c = pl.program_id(1)
    hal = CONV_HALO

    @pl.when(c == 0)
    def _():
        it = i % tiles_per_seq
        gp = gpre_ref[...]
        f_ref[0:tm, :] = _rms(x_ref[...], gp).astype(BF16)
        edge = jnp.concatenate([_rms(xn_ref[...], gp), _rms(xp_ref[...], gp)], axis=0)
        row = lax.broadcasted_iota(jnp.int32, (hal, 1), 0)
        next_row = jnp.where(it < tiles_per_seq - 1, 0, -1)
        prev_row = jnp.where(it > 0, hal - 1, -1)
        f_ref[tm:, :] = jnp.where((row == next_row) | (row == prev_row), edge, 0.0).astype(BF16)

    g = jnp.dot(f_ref[...], wg_ref[...], preferred_element_type=F32)
    up = jnp.dot(f_ref[0:tm, :], wu_ref[...], preferred_element_type=F32)
    cw = cw_ref[...]
    g_prev = pltpu.roll(g, 1, axis=0)[0:tm, :]
    g_next = pltpu.roll(g, tm + hal - 1, axis=0)[0:tm, :]
    gate = g_prev * cw[0:1, :] + g[0:tm, :] * cw[1:2, :] + g_next * cw[2:3, :] + cb_ref[...]
    h_ref[...] = (_gelu_tanh(gate) * up).astype(BF16)


def _gate_up(x1, g_pre, w_gate, w_up, conv_w, conv_b, seq, tm, tc):
    n, d = x1.shape
    dff = w_gate.shape[1]
    hal = CONV_HALO
    nb8 = tm // 8
    tiles_per_seq = seq // tm
    return pl.pallas_call(
        functools.partial(_gate_up_kernel, tm=tm, tiles_per_seq=tiles_per_seq),
        grid=(n // tm, dff // tc),
        in_specs=[
            pl.BlockSpec((tm, d), lambda i, c: (i, 0)),
            pl.BlockSpec((8, d), lambda i, c: (jnp.maximum(i * nb8 - 1, 0), 0)),
            pl.BlockSpec((8, d), lambda i, c: (jnp.minimum((i + 1) * nb8, n // 8 - 1), 0)),
            pl.BlockSpec((1, d), lambda i, c: (0, 0)),
            pl.BlockSpec((d, tc), lambda i, c: (0, c)),
            pl.BlockSpec((d, tc), lambda i, c: (0, c)),
            pl.BlockSpec((3, tc), lambda i, c: (0, c)),
            pl.BlockSpec((1, tc), lambda i, c: (0, c)),
        ],
        out_specs=pl.BlockSpec((tm, tc), lambda i, c: (i, c)),
        out_shape=jax.ShapeDtypeStruct((n, dff), BF16),
        scratch_shapes=[pltpu.VMEM((tm + hal, d), BF16)],
        compiler_params=pltpu.CompilerParams(
            dimension_semantics=("parallel", "arbitrary"), vmem_limit_bytes=V7X_VMEM_LIMIT),
        name="ffn_gate_up",
    )(x1, x1, x1, g_pre, w_gate, w_up, conv_w, conv_b)


def _down_kernel(h_ref, w_ref, x_ref, g_ref, o_ref, *, tm):
    half = tm // 2
    for r0 in (0, half):
        rows = slice(r0, r0 + half)
        y = jnp.dot(h_ref[rows, :], w_ref[...], preferred_element_type=F32)
        o_ref[rows, :] = x_ref[rows, :] + _rms(y, g_ref[...])


def _down(h, w_down, x1, g_post, tm):
    n, d = x1.shape
    dff = h.shape[1]
    return pl.pallas_call(
        functools.partial(_down_kernel, tm=tm),
        grid=(n // tm,),
        in_specs=[
            pl.BlockSpec((tm, dff), lambda i: (i, 0)),
            pl.BlockSpec((dff, d), lambda i: (0, 0), pipeline_mode=pl.Buffered(1)),
            pl.BlockSpec((tm, d), lambda i: (i, 0)),
            pl.BlockSpec((1, d), lambda i: (0, 0)),
        ],
        out_specs=pl.BlockSpec((tm, d), lambda i: (i, 0)),
        out_shape=jax.ShapeDtypeStruct((n, d), F32),
        compiler_params=pltpu.CompilerParams(
            dimension_semantics=("parallel",), vmem_limit_bytes=V7X_VMEM_LIMIT),
        name="ffn_down",
    )(h, w_down, x1, g_post)


def kernel(x, norm_pre_mix, w_in, w_pool, pool_scale, attn_sink, w_out, norm_post_mix, norm_pre_ffn,
           w_gate, w_up, conv_w, conv_b, w_down, norm_post_ffn):
    b, s, d = x.shape
    depth = w_in.shape[0]
    pool_width = pool_scale.shape[1]
    n_heads = attn_sink.shape[1]
    attn_width = n_heads * HEAD_DIM
    x2 = x.reshape(b * s, d)
    for l in range(depth):
        y_pool, qt, k, vt, w_out_bf, w_gate_bf = _in_proj_pool(
            x2, norm_pre_mix[l][None, :], w_in[l].astype(BF16), w_pool[l].astype(BF16), pool_scale[l][None, :],
            w_out[l], w_gate[l], s, pool_width, attn_width, tm=ROW_TILE)
        y_attn, w_up_bf, w_down_bf = _attention(qt, k.reshape(b, s, -1), vt, attn_sink[l], w_up[l], w_down[l],
                                                tq=ROW_TILE)
        x2 = _out_proj(y_pool, y_attn.reshape(b * s, attn_width), w_out_bf, x2, norm_post_mix[l][None, :], tm=ROW_TILE)
        hmid = _gate_up(x2, norm_pre_ffn[l][None, :], w_gate_bf, w_up_bf, conv_w[l], conv_b[l][None, :], seq=s,
                        tm=ROW_TILE, tc=FFN_CHUNK)
        x2 = _down(hmid, w_down_bf, x2, norm_post_ffn[l][None, :], tm=DOWN_TILE)
    return x2.reshape(b, s, d)
```

```python
import functools

import jax
import jax.numpy as jnp
import numpy as np
from jax import lax
from jax.experimental import pallas as pl
from jax.experimental.pallas import tpu as pltpu

F32 = jnp.float32
BF16 = jnp.bfloat16

EPS = 1e-6
LOG2E = 1.4426950408889634
POOL_WINDOWS = (2, 4, 8, 16)
HEAD_DIM = 64
Q_PER_KV = 4
WINDOW = 128
QBLK = 128
KWIN = 3 * QBLK
SUM_ROWS = 16
IN_HALO = 16
CONV_HALO = 16
FFN_CHUNK = 512
ROW_TILE = 1024
DOWN_TILE = 512
V7X_VMEM_LIMIT = 62 * 1024 * 1024


def _rms(x, g):
    return x * lax.rsqrt(jnp.mean(x * x, axis=-1, keepdims=True) + EPS) * g


def _roll_rows(a, shift):
    return pltpu.roll(a, shift % a.shape[0], axis=0)


def _inproj_pool_kernel(x_ref, xp_ref, xn_ref, g_ref, w_ref, wp_ref, ps_ref, ca_ref, cb_ref,
                        yp_ref, qt_ref, k_ref, vt_ref, ca_out_ref, cb_out_ref, h_ref,
                        *, tm, seq, pool_width, attn_width):
    ca_out_ref[...] = ca_ref[...].astype(BF16)
    cb_out_ref[...] = cb_ref[...].astype(BF16)
    i = pl.program_id(0)
    tiles_per_seq = seq // tm
    it = i % tiles_per_seq
    hal = IN_HALO
    half = tm // 2
    rows = tm + 2 * hal
    gw = pool_width // len(POOL_WINDOWS)
    gp = g_ref[...]

    h_ref[0:hal, :] = jnp.where(it > 0, _rms(xp_ref[...], gp), 0.0).astype(BF16)
    h_ref[hal:hal + half, :] = _rms(x_ref[0:half, :], gp).astype(BF16)
    h_ref[hal + half:hal + tm, :] = _rms(x_ref[half:tm, :], gp).astype(BF16)
    h_ref[hal + tm:, :] = jnp.where(it < tiles_per_seq - 1, _rms(xn_ref[...], gp), 0.0).astype(BF16)

    w_u = w_ref[:, :pool_width]
    u = jnp.concatenate([jnp.dot(h_ref[0:hal + half, :], w_u, preferred_element_type=F32),
                         jnp.dot(h_ref[hal + half:rows, :], w_u, preferred_element_type=F32)], axis=0)
    w_q = w_ref[:, pool_width:pool_width + attn_width]
    w_kv = w_ref[:, pool_width + attn_width:]
    kv_width = k_ref.shape[1]
    grp_width = Q_PER_KV * HEAD_DIM
    for r0 in (0, half):
        hm = h_ref[hal + r0:hal + r0 + half, :]
        q = jnp.dot(hm, w_q, preferred_element_type=F32) * (HEAD_DIM ** -0.5 * LOG2E)
        for jb in range(half // QBLK):
            for hk in range(attn_width // grp_width):
                t = q[jb * QBLK:(jb + 1) * QBLK, hk * grp_width:(hk + 1) * grp_width].T
                qt = jnp.concatenate([t[g * HEAD_DIM:(g + 1) * HEAD_DIM, :] for g in range(Q_PER_KV)], axis=1)
                qt_ref[r0 // QBLK + jb, hk] = qt.astype(BF16)
        kv = jnp.dot(hm, w_kv, preferred_element_type=F32)
        k_ref[r0:r0 + half, :] = kv[:, :kv_width].astype(BF16)
        vt_ref[:, r0:r0 + half] = kv[:, kv_width:].T.astype(BF16)

    a1 = u + _roll_rows(u, 1)
    a1r = a1[:, gw:]
    a2 = _roll_rows(a1r, -1) + _roll_rows(a1r, 1)
    a2r = a2[:, gw:]
    a4 = _roll_rows(a2r, -2) + _roll_rows(a2r, 2)
    a4r = a4[:, gw:]
    a8 = _roll_rows(a4r, -4) + _roll_rows(a4r, 4)
    main = slice(hal, hal + tm)
    wins = (a1[main, :gw], a2[main, :gw], a4[main, :gw], a8[main, :])
    t = it * tm + lax.broadcasted_iota(jnp.int32, (tm, 1), 0)
    for g, w in enumerate(POOL_WINDOWS):
        lo = jnp.maximum(t - w // 2, 0)
        hi = jnp.minimum(t + w // 2, seq)
        inv_cnt = 1.0 / (hi - lo).astype(F32)
        cols = slice(g * gw, (g + 1) * gw)
        d = (wins[g] * inv_cnt - u[main, cols]).astype(BF16)
        y = jnp.dot(d, wp_ref[g], preferred_element_type=F32) * ps_ref[:, cols]
        yp_ref[:, cols] = y.astype(BF16)


def _in_proj_pool(x2, g, w_in, w_pool, pool_scale, cast_a, cast_b, seq, pool_width, attn_width, tm):
    n, d = x2.shape
    in_width = w_in.shape[1]
    ng = len(POOL_WINDOWS)
    gw = pool_width // ng
    hal = IN_HALO
    nh = tm // hal
    steps = n // tm
    kv_width = (in_width - pool_width - attn_width) // 2
    n_kv = attn_width // (Q_PER_KV * HEAD_DIM)
    resident = dict(pipeline_mode=pl.Buffered(1))
    slab_a = pl.BlockSpec((cast_a.shape[0] // steps, cast_a.shape[1]), lambda i: (i, 0))
    slab_b = pl.BlockSpec((cast_b.shape[0] // steps, cast_b.shape[1]), lambda i: (i, 0))
    return pl.pallas_call(
        functools.partial(_inproj_pool_kernel, tm=tm, seq=seq, pool_width=pool_width, attn_width=attn_width),
        grid=(n // tm,),
        in_specs=[
            pl.BlockSpec((tm, d), lambda i: (i, 0)),
            pl.BlockSpec((hal, d), lambda i: (jnp.maximum(i * nh - 1, 0), 0)),
            pl.BlockSpec((hal, d), lambda i: (jnp.minimum((i + 1) * nh, n // hal - 1), 0)),
            pl.BlockSpec((1, d), lambda i: (0, 0)),
            pl.BlockSpec((d, in_width), lambda i: (0, 0), **resident),
            pl.BlockSpec((ng, gw, gw), lambda i: (0, 0, 0), **resident),
            pl.BlockSpec((1, pool_width), lambda i: (0, 0)),
            slab_a,
            slab_b,
        ],
        out_specs=[
            pl.BlockSpec((tm, pool_width), lambda i: (i, 0)),
            pl.BlockSpec((tm // QBLK, n_kv, HEAD_DIM, Q_PER_KV * QBLK), lambda i: (i, 0, 0, 0)),
            pl.BlockSpec((tm, kv_width), lambda i: (i, 0)),
            pl.BlockSpec((kv_width, tm), lambda i: (0, i)),
            slab_a,
            slab_b,
        ],
        out_shape=[
            jax.ShapeDtypeStruct((n, pool_width), BF16),
            jax.ShapeDtypeStruct((n // QBLK, n_kv, HEAD_DIM, Q_PER_KV * QBLK), BF16),
            jax.ShapeDtypeStruct((n, kv_width), BF16),
            jax.ShapeDtypeStruct((kv_width, n), BF16),
            jax.ShapeDtypeStruct(cast_a.shape, BF16),
            jax.ShapeDtypeStruct(cast_b.shape, BF16),
        ],
        scratch_shapes=[pltpu.VMEM((tm + 2 * hal, d), BF16)],
        compiler_params=pltpu.CompilerParams(
            dimension_semantics=("parallel",), vmem_limit_bytes=V7X_VMEM_LIMIT),
        name="in_proj_pool",
    )(x2, x2, x2, g, w_in, w_pool, pool_scale, cast_a, cast_b)


def _attn_kernel(sink_ref, qt_ref, k_ref, vt_ref, ca_ref, cb_ref, o_ref, ca_out_ref, cb_out_ref,
                 bias_ref, sinkrow_ref, st_ref, *, tq, seq, slopes):
    ca_out_ref[...] = ca_ref[...].astype(BF16)
    cb_out_ref[...] = cb_ref[...].astype(BF16)
    bi = pl.program_id(0)
    i = pl.program_id(1)
    grp = Q_PER_KV
    n_heads = len(slopes)
    n_kv = n_heads // grp

    @pl.when((bi == 0) & (i == 0))
    def _():
        kj = lax.broadcasted_iota(jnp.int32, (KWIN, QBLK), 0)
        qi = lax.broadcasted_iota(jnp.int32, (KWIN, QBLK), 1)
        for v in range(3):
            dist_i = jnp.abs(v * QBLK + qi - kj)
            valid = dist_i <= WINDOW
            dist = dist_i.astype(F32)
            for h in range(n_heads):
                cols = slice((h % grp) * QBLK, (h % grp + 1) * QBLK)
                bias = jnp.where(valid, -(slopes[h] * dist) * LOG2E, -jnp.inf)
                bias_ref[v, h // grp, :, cols] = bias.astype(BF16)
        for h in range(n_heads):
            cols = slice((h % grp) * QBLK, (h % grp + 1) * QBLK)
            sinkrow_ref[h // grp, :, cols] = jnp.full((1, QBLK), sink_ref[h] * LOG2E, F32)

    def sub_block(j, carry):
        q0 = i * tq + j * QBLK
        k0 = pl.multiple_of(jnp.clip(q0 - WINDOW, 0, seq - KWIN), QBLK)
        variant = (q0 - k0) // QBLK
        rows = pl.ds(pl.multiple_of(j * QBLK, QBLK), QBLK)
        par = (i + j) % 2
        m_all = []
        for hk in range(n_kv):
            k = k_ref[0, pl.ds(k0, KWIN), hk * HEAD_DIM:(hk + 1) * HEAD_DIM]
            st = jnp.dot(k, qt_ref[j, hk], preferred_element_type=F32)
            st_ref[par, hk] = st.astype(BF16) + bias_ref[variant, hk]
            m_all.append(jnp.maximum(jnp.max(st_ref[par, hk], axis=0, keepdims=True),
                                     sinkrow_ref[hk].astype(BF16)))
        ones = jnp.ones((SUM_ROWS, KWIN), BF16)
        for hk in range(n_kv):
            e = jnp.exp2(st_ref[par, hk] - m_all[hk])
            vt = jnp.concatenate([vt_ref[hk * HEAD_DIM:(hk + 1) * HEAD_DIM, pl.ds(k0, KWIN)], ones], axis=0)
            ot = jnp.dot(vt, e, preferred_element_type=F32)
            den = ot[HEAD_DIM:HEAD_DIM + 1, :] + jnp.exp2(sinkrow_ref[hk] - m_all[hk].astype(F32))
            ot = ot[:HEAD_DIM, :] * (1.0 / den)
            o4 = jnp.concatenate([ot[:, g * QBLK:(g + 1) * QBLK].T for g in range(grp)], axis=1)
            o_ref[0, rows, hk * grp * HEAD_DIM:(hk + 1) * grp * HEAD_DIM] = o4.astype(BF16)
        return carry

    for j in range(tq // QBLK):
        sub_block(j, 0)


def _attention(qt, k3, vt, sink, cast_a, cast_b, tq):
    b, s, kv_width = k3.shape
    n_kv = qt.shape[1]
    attn_width = n_kv * Q_PER_KV * HEAD_DIM
    steps = b * (s // tq)
    slab_a = pl.BlockSpec((cast_a.shape[0] // steps, cast_a.shape[1]), lambda bi, i: (bi * (s // tq) + i, 0))
    slab_b = pl.BlockSpec((cast_b.shape[0] // steps, cast_b.shape[1]), lambda bi, i: (bi * (s // tq) + i, 0))
    n_heads = attn_width // HEAD_DIM
    idx = np.arange(1, n_heads + 1, dtype=np.float32)
    slopes = tuple(float(v) for v in np.power(2.0, -8.0 * idx / n_heads).astype(np.float32))
    return pl.pallas_call(
        functools.partial(_attn_kernel, tq=tq, seq=s, slopes=slopes),
        grid=(b, s // tq),
        in_specs=[
            pl.BlockSpec(memory_space=pltpu.SMEM),
            pl.BlockSpec((tq // QBLK, n_kv, HEAD_DIM, Q_PER_KV * QBLK), lambda bi, i: (bi * (s // tq) + i, 0, 0, 0)),
            pl.BlockSpec((1, s, kv_width), lambda bi, i: (bi, 0, 0)),
            pl.BlockSpec((kv_width, s), lambda bi, i: (0, bi)),
            slab_a,
            slab_b,
        ],
        out_specs=[pl.BlockSpec((1, tq, attn_width), lambda bi, i: (bi, i, 0)), slab_a, slab_b],
        out_shape=[jax.ShapeDtypeStruct((b, s, attn_width), BF16),
                   jax.ShapeDtypeStruct(cast_a.shape, BF16),
                   jax.ShapeDtypeStruct(cast_b.shape, BF16)],
        scratch_shapes=[
            pltpu.VMEM((3, n_kv, KWIN, Q_PER_KV * QBLK), BF16),
            pltpu.VMEM((n_kv, 1, Q_PER_KV * QBLK), F32),
            pltpu.VMEM((2, n_kv, KWIN, Q_PER_KV * QBLK), BF16),
        ],
        compiler_params=pltpu.CompilerParams(
            dimension_semantics=("arbitrary", "arbitrary"), vmem_limit_bytes=V7X_VMEM_LIMIT),
        name="window_attn",
    )(sink, qt, k3, vt, cast_a, cast_b)


def _outproj_kernel(yp_ref, ya_ref, w_ref, x_ref, g_ref, o_ref, *, pool_width, tm):
    half = tm // 2
    for r0 in (0, half):
        rows = slice(r0, r0 + half)
        mix = jnp.dot(yp_ref[rows, :], w_ref[:pool_width, :], preferred_element_type=F32)
        mix = mix + jnp.dot(ya_ref[rows, :], w_ref[pool_width:, :], preferred_element_type=F32)
        o_ref[rows, :] = x_ref[rows, :] + _rms(mix, g_ref[...])


def _out_proj(yp, ya, w_out, x2, g, tm):
    n, d = x2.shape
    pw = yp.shape[1]
    aw = ya.shape[1]
    return pl.pallas_call(
        functools.partial(_outproj_kernel, pool_width=pw, tm=tm),
        grid=(n // tm,),
        in_specs=[
            pl.BlockSpec((tm, pw), lambda i: (i, 0)),
            pl.BlockSpec((tm, aw), lambda i: (i, 0)),
            pl.BlockSpec((pw + aw, d), lambda i: (0, 0), pipeline_mode=pl.Buffered(1)),
            pl.BlockSpec((tm, d), lambda i: (i, 0)),
            pl.BlockSpec((1, d), lambda i: (0, 0)),
        ],
        out_specs=pl.BlockSpec((tm, d), lambda i: (i, 0)),
        out_shape=jax.ShapeDtypeStruct((n, d), F32),
        compiler_params=pltpu.CompilerParams(
            dimension_semantics=("parallel",), vmem_limit_bytes=V7X_VMEM_LIMIT),
        name="out_proj",
    )(yp, ya, w_out, x2, g)


def _gelu_tanh(x):
    return 0.5 * x * (1.0 + jnp.tanh(np.sqrt(2.0 / np.pi).astype(np.float32) * (x + 0.044715 * (x * x * x))))


def _gate_up_kernel(x_ref, xp_ref, xn_ref, gpre_ref, wg_ref, wu_ref, cw_ref, cb_ref, h_ref, f_ref,
                    *, tm, tiles_per_seq):
    i = pl.program_id(0)
    c = pl.program_id(1)
    hal = CONV_HALO

    @pl.when(c == 0)
    def _():
        it = i % tiles_per_seq
        gp = gpre_ref[...]
        f_ref[0:tm, :] = _rms(x_ref[...], gp).astype(BF16)
        edge = jnp.concatenate([_rms(xn_ref[...], gp), _rms(xp_ref[...], gp)], axis=0)
        row = lax.broadcasted_iota(jnp.int32, (hal, 1), 0)
        next_row = jnp.where(it < tiles_per_seq - 1, 0, -1)
        prev_row = jnp.where(it > 0, hal - 1, -1)
        f_ref[tm:, :] = jnp.where((row == next_row) | (row == prev_row), edge, 0.0).astype(BF16)

    g = jnp.dot(f_ref[...], wg_ref[...], preferred_element_type=F32)
    up = jnp.dot(f_ref[0:tm, :], wu_ref[...], preferred_element_type=F32)
    cw = cw_ref[...]
    g_prev = pltpu.roll(g, 1, axis=0)[0:tm, :]
    g_next = pltpu.roll(g, tm + hal - 1, axis=0)[0:tm, :]
    gate = g_prev * cw[0:1, :] + g[0:tm, :] * cw[1:2, :] + g_next * cw[2:3, :] + cb_ref[...]
    h_ref[...] = (_gelu_tanh(gate) * up).astype(BF16)


def _gate_up(x1, g_pre, w_gate, w_up, conv_w, conv_b, seq, tm, tc):
    n, d = x1.shape
    dff = w_gate.shape[1]
    hal = CONV_HALO
    nb8 = tm // 8
    tiles_per_seq = seq // tm
    return pl.pallas_call(
        functools.partial(_gate_up_kernel, tm=tm, tiles_per_seq=tiles_per_seq),
        grid=(n // tm, dff // tc),
        in_specs=[
            pl.BlockSpec((tm, d), lambda i, c: (i, 0)),
            pl.BlockSpec((8, d), lambda i, c: (jnp.maximum(i * nb8 - 1, 0), 0)),
            pl.BlockSpec((8, d), lambda i, c: (jnp.minimum((i + 1) * nb8, n // 8 - 1), 0)),
            pl.BlockSpec((1, d), lambda i, c: (0, 0)),
            pl.BlockSpec((d, tc), lambda i, c: (0, c)),
            pl.BlockSpec((d, tc), lambda i, c: (0, c)),
            pl.BlockSpec((3, tc), lambda i, c: (0, c)),
            pl.BlockSpec((1, tc), lambda i, c: (0, c)),
        ],
        out_specs=pl.BlockSpec((tm, tc), lambda i, c: (i, c)),
        out_shape=jax.ShapeDtypeStruct((n, dff), BF16),
        scratch_shapes=[pltpu.VMEM((tm + hal, d), BF16)],
        compiler_params=pltpu.CompilerParams(
            dimension_semantics=("parallel", "arbitrary"), vmem_limit_bytes=V7X_VMEM_LIMIT),
        name="ffn_gate_up",
    )(x1, x1, x1, g_pre, w_gate, w_up, conv_w, conv_b)


def _down_kernel(h_ref, w_ref, x_ref, g_ref, o_ref, *, tm):
    half = tm // 2
    for r0 in (0, half):
        rows = slice(r0, r0 + half)
        y = jnp.dot(h_ref[rows, :], w_ref[...], preferred_element_type=F32)
        o_ref[rows, :] = x_ref[rows, :] + _rms(y, g_ref[...])


def _down(h, w_down, x1, g_post, tm):
    n, d = x1.shape
    dff = h.shape[1]
    return pl.pallas_call(
        functools.partial(_down_kernel, tm=tm),
        grid=(n // tm,),
        in_specs=[
            pl.BlockSpec((tm, dff), lambda i: (i, 0)),
            pl.BlockSpec((dff, d), lambda i: (0, 0), pipeline_mode=pl.Buffered(1)),
            pl.BlockSpec((tm, d), lambda i: (i, 0)),
            pl.BlockSpec((1, d), lambda i: (0, 0)),
        ],
        out_specs=pl.BlockSpec((tm, d), lambda i: (i, 0)),
        out_shape=jax.ShapeDtypeStruct((n, d), F32),
        compiler_params=pltpu.CompilerParams(
            dimension_semantics=("parallel",), vmem_limit_bytes=V7X_VMEM_LIMIT),
        name="ffn_down",
    )(h, w_down, x1, g_post)


def kernel(x, norm_pre_mix, w_in, w_pool, pool_scale, attn_sink, w_out, norm_post_mix, norm_pre_ffn,
           w_gate, w_up, conv_w, conv_b, w_down, norm_post_ffn):
    b, s, d = x.shape
    depth = w_in.shape[0]
    pool_width = pool_scale.shape[1]
    n_heads = attn_sink.shape[1]
    attn_width = n_heads * HEAD_DIM
    x2 = x.reshape(b * s, d)
    for l in range(depth):
        y_pool, qt, k, vt, w_out_bf, w_gate_bf = _in_proj_pool(
            x2, norm_pre_mix[l][None, :], w_in[l].astype(BF16), w_pool[l].astype(BF16), pool_scale[l][None, :],
            w_out[l], w_gate[l], s, pool_width, attn_width, tm=ROW_TILE)
        y_attn, w_up_bf, w_down_bf = _attention(qt, k.reshape(b, s, -1), vt, attn_sink[l], w_up[l], w_down[l],
                                                tq=ROW_TILE)
        x2 = _out_proj(y_pool, y_attn.reshape(b * s, attn_width), w_out_bf, x2, norm_post_mix[l][None, :], tm=ROW_TILE)
        hmid = _gate_up(x2, norm_pre_ffn[l][None, :], w_gate_bf, w_up_bf, conv_w[l], conv_b[l][None, :], seq=s,
                        tm=ROW_TILE, tc=FFN_CHUNK)
        x2 = _down(hmid, w_down_bf, x2, norm_post_ffn[l][None, :], tm=DOWN_TILE)
    return x2.reshape(b, s, d)
```

```python
import functools

import jax
import jax.numpy as jnp
import numpy as np
from jax import lax
from jax.experimental import pallas as pl
from jax.experimental.pallas import tpu as pltpu

F32 = jnp.float32
BF16 = jnp.bfloat16

EPS = 1e-6
LOG2E = 1.4426950408889634
POOL_WINDOWS = (2, 4, 8, 16)
HEAD_DIM = 64
Q_PER_KV = 4
WINDOW = 128
QBLK = 128
KWIN = 3 * QBLK
SUM_ROWS = 16
IN_HALO = 16
CONV_HALO = 16
FFN_CHUNK = 512
ROW_TILE = 1024
DOWN_TILE = 512
V7X_VMEM_LIMIT = 62 * 1024 * 1024


def _rms(x, g):
    return x * lax.rsqrt(jnp.mean(x * x, axis=-1, keepdims=True) + EPS) * g


def _roll_rows(a, shift):
    return pltpu.roll(a, shift % a.shape[0], axis=0)


def _inproj_pool_kernel(x_ref, xp_ref, xn_ref, g_ref, w_ref, wp_ref, ps_ref, ca_ref, cb_ref,
                        yp_ref, qt_ref, k_ref, vt_ref, ca_out_ref, cb_out_ref, h_ref,
                        *, tm, seq, pool_width, attn_width):
    ca_out_ref[...] = ca_ref[...].astype(BF16)
    cb_out_ref[...] = cb_ref[...].astype(BF16)
    i = pl.program_id(0)
    tiles_per_seq = seq // tm
    it = i % tiles_per_seq
    hal = IN_HALO
    half = tm // 2
    rows = tm + 2 * hal
    gw = pool_width // len(POOL_WINDOWS)
    gp = g_ref[...]

    h_ref[0:hal, :] = jnp.where(it > 0, _rms(xp_ref[...], gp), 0.0).astype(BF16)
    h_ref[hal:hal + half, :] = _rms(x_ref[0:half, :], gp).astype(BF16)
    h_ref[hal + half:hal + tm, :] = _rms(x_ref[half:tm, :], gp).astype(BF16)
    h_ref[hal + tm:, :] = jnp.where(it < tiles_per_seq - 1, _rms(xn_ref[...], gp), 0.0).astype(BF16)

    w_u = w_ref[:, :pool_width]
    u = jnp.concatenate([jnp.dot(h_ref[0:hal + half, :], w_u, preferred_element_type=F32),
                         jnp.dot(h_ref[hal + half:rows, :], w_u, preferred_element_type=F32)], axis=0)
    w_q = w_ref[:, pool_width:pool_width + attn_width]
    w_kv = w_ref[:, pool_width + attn_width:]
    kv_width = k_ref.shape[1]
    grp_width = Q_PER_KV * HEAD_DIM
    for r0 in (0, half):
        hm = h_ref[hal + r0:hal + r0 + half, :]
        q = jnp.dot(hm, w_q, preferred_element_type=F32) * (HEAD_DIM ** -0.5 * LOG2E)
        for jb in range(half // QBLK):
            for hk in range(attn_width // grp_width):
                t = q[jb * QBLK:(jb + 1) * QBLK, hk * grp_width:(hk + 1) * grp_width].T
                qt = jnp.concatenate([t[g * HEAD_DIM:(g + 1) * HEAD_DIM, :] for g in range(Q_PER_KV)], axis=1)
                qt_ref[r0 // QBLK + jb, hk] = qt.astype(BF16)
        kv = jnp.dot(hm, w_kv, preferred_element_type=F32)
        k_ref[r0:r0 + half, :] = kv[:, :kv_width].astype(BF16)
        vt_ref[:, r0:r0 + half] = kv[:, kv_width:].T.astype(BF16)

    a1 = u + _roll_rows(u, 1)
    a1r = a1[:, gw:]
    a2 = _roll_rows(a1r, -1) + _roll_rows(a1r, 1)
    a2r = a2[:, gw:]
    a4 = _roll_rows(a2r, -2) + _roll_rows(a2r, 2)
    a4r = a4[:, gw:]
    a8 = _roll_rows(a4r, -4) + _roll_rows(a4r, 4)
    main = slice(hal, hal + tm)
    wins = (a1[main, :gw], a2[main, :gw], a4[main, :gw], a8[main, :])
    t = it * tm + lax.broadcasted_iota(jnp.int32, (tm, 1), 0)
    for g, w in enumerate(POOL_WINDOWS):
        lo = jnp.maximum(t - w // 2, 0)
        hi = jnp.minimum(t + w // 2, seq)
        inv_cnt = 1.0 / (hi - lo).astype(F32)
        cols = slice(g * gw, (g + 1) * gw)
        d = (wins[g] * inv_cnt - u[main, cols]).astype(BF16)
        y = jnp.dot(d, wp_ref[g], preferred_element_type=F32) * ps_ref[:, cols]
        yp_ref[:, cols] = y.astype(BF16)


def _in_proj_pool(x2, g, w_in, w_pool, pool_scale, cast_a, cast_b, seq, pool_width, attn_width, tm):
    n, d = x2.shape
    in_width = w_in.shape[1]
    ng = len(POOL_WINDOWS)
    gw = pool_width // ng
    hal = IN_HALO
    nh = tm // hal
    steps = n // tm
    kv_width = (in_width - pool_width - attn_width) // 2
    n_kv = attn_width // (Q_PER_KV * HEAD_DIM)
    resident = dict(pipeline_mode=pl.Buffered(1))
    slab_a = pl.BlockSpec((cast_a.shape[0] // steps, cast_a.shape[1]), lambda i: (i, 0))
    slab_b = pl.BlockSpec((cast_b.shape[0] // steps, cast_b.shape[1]), lambda i: (i, 0))
    return pl.pallas_call(
        functools.partial(_inproj_pool_kernel, tm=tm, seq=seq, pool_width=pool_width, attn_width=attn_width),
        grid=(n // tm,),
        in_specs=[
            pl.BlockSpec((tm, d), lambda i: (i, 0)),
            pl.BlockSpec((hal, d), lambda i: (jnp.maximum(i * nh - 1, 0), 0)),
            pl.BlockSpec((hal, d), lambda i: (jnp.minimum((i + 1) * nh, n // hal - 1), 0)),
            pl.BlockSpec((1, d), lambda i: (0, 0)),
            pl.BlockSpec((d, in_width), lambda i: (0, 0), **resident),
            pl.BlockSpec((ng, gw, gw), lambda i: (0, 0, 0), **resident),
            pl.BlockSpec((1, pool_width), lambda i: (0, 0)),
            slab_a,
            slab_b,
        ],
        out_specs=[
            pl.BlockSpec((tm, pool_width), lambda i: (i, 0)),
            pl.BlockSpec((tm // QBLK, n_kv, HEAD_DIM, Q_PER_KV * QBLK), lambda i: (i, 0, 0, 0)),
            pl.BlockSpec((tm, kv_width), lambda i: (i, 0)),
            pl.BlockSpec((kv_width, tm), lambda i: (0, i)),
            slab_a,
            slab_b,
        ],
        out_shape=[
            jax.ShapeDtypeStruct((n, pool_width), BF16),
            jax.ShapeDtypeStruct((n // QBLK, n_kv, HEAD_DIM, Q_PER_KV * QBLK), BF16),
            jax.ShapeDtypeStruct((n, kv_width), BF16),
            jax.ShapeDtypeStruct((kv_width, n), BF16),
            jax.ShapeDtypeStruct(cast_a.shape, BF16),
            jax.ShapeDtypeStruct(cast_b.shape, BF16),
        ],
        scratch_shapes=[pltpu.VMEM((tm + 2 * hal, d), BF16)],
        compiler_params=pltpu.CompilerParams(
            dimension_semantics=("parallel",), vmem_limit_bytes=V7X_VMEM_LIMIT),
        name="in_proj_pool",
    )(x2, x2, x2, g, w_in, w_pool, pool_scale, cast_a, cast_b)


def _attn_kernel(sink_ref, qt_ref, k_ref, vt_ref, ca_ref, cb_ref, o_ref, ca_out_ref, cb_out_ref,
                 bias_ref, sinkrow_ref, st_ref, *, tq, seq, slopes):
    ca_out_ref[...] = ca_ref[...].astype(BF16)
    cb_out_ref[...] = cb_ref[...].astype(BF16)
    bi = pl.program_id(0)
    i = pl.program_id(1)
    grp = Q_PER_KV
    n_heads = len(slopes)
    n_kv = n_heads // grp

    @pl.when((bi == 0) & (i == 0))
    def _():
        kj = lax.broadcasted_iota(jnp.int32, (KWIN, QBLK), 0)
        qi = lax.broadcasted_iota(jnp.int32, (KWIN, QBLK), 1)
        for v in range(3):
            dist_i = jnp.abs(v * QBLK + qi - kj)
            valid = dist_i <= WINDOW
            dist = dist_i.astype(F32)
            for h in range(n_heads):
                cols = slice((h % grp) * QBLK, (h % grp + 1) * QBLK)
                bias_ref[v, h // grp, :, cols] = jnp.where(valid, -(slopes[h] * dist) * LOG2E, -jnp.inf)
        for h in range(n_heads):
            cols = slice((h % grp) * QBLK, (h % grp + 1) * QBLK)
            sinkrow_ref[h // grp, :, cols] = jnp.full((1, QBLK), sink_ref[h] * LOG2E, F32)

    def sub_block(j, carry):
        q0 = i * tq + j * QBLK
        k0 = pl.multiple_of(jnp.clip(q0 - WINDOW, 0, seq - KWIN), QBLK)
        variant = (q0 - k0) // QBLK
        rows = pl.ds(pl.multiple_of(j * QBLK, QBLK), QBLK)
        par = (i + j) % 2
        m_all = []
        for hk in range(n_kv):
            k = k_ref[0, pl.ds(k0, KWIN), hk * HEAD_DIM:(hk + 1) * HEAD_DIM]
            st = jnp.dot(k, qt_ref[j, hk], preferred_element_type=F32)
            st_ref[par, hk] = st + bias_ref[variant, hk]
            m_all.append(jnp.maximum(jnp.max(st_ref[par, hk], axis=0, keepdims=True), sinkrow_ref[hk]))
        ones = jnp.ones((SUM_ROWS, KWIN), BF16)
        for hk in range(n_kv):
            e = jnp.exp2(st_ref[par, hk] - m_all[hk]).astype(BF16)
            vt = jnp.concatenate([vt_ref[hk * HEAD_DIM:(hk + 1) * HEAD_DIM, pl.ds(k0, KWIN)], ones], axis=0)
            ot = jnp.dot(vt, e, preferred_element_type=F32)
            den = ot[HEAD_DIM:HEAD_DIM + 1, :] + jnp.exp2(sinkrow_ref[hk] - m_all[hk])
            ot = ot[:HEAD_DIM, :] * (1.0 / den)
            o4 = jnp.concatenate([ot[:, g * QBLK:(g + 1) * QBLK].T for g in range(grp)], axis=1)
            o_ref[0, rows, hk * grp * HEAD_DIM:(hk + 1) * grp * HEAD_DIM] = o4.astype(BF16)
        return carry

    for j in range(tq // QBLK):
        sub_block(j, 0)


def _attention(qt, k3, vt, sink, cast_a, cast_b, tq):
    b, s, kv_width = k3.shape
    n_kv = qt.shape[1]
    attn_width = n_kv * Q_PER_KV * HEAD_DIM
    steps = b * (s // tq)
    slab_a = pl.BlockSpec((cast_a.shape[0] // steps, cast_a.shape[1]), lambda bi, i: (bi * (s // tq) + i, 0))
    slab_b = pl.BlockSpec((cast_b.shape[0] // steps, cast_b.shape[1]), lambda bi, i: (bi * (s // tq) + i, 0))
    n_heads = attn_width // HEAD_DIM
    idx = np.arange(1, n_heads + 1, dtype=np.float32)
    slopes = tuple(float(v) for v in np.power(2.0, -8.0 * idx / n_heads).astype(np.float32))
    return pl.pallas_call(
        functools.partial(_attn_kernel, tq=tq, seq=s, slopes=slopes),
        grid=(b, s // tq),
        in_specs=[
            pl.BlockSpec(memory_space=pltpu.SMEM),
            pl.BlockSpec((tq // QBLK, n_kv, HEAD_DIM, Q_PER_KV * QBLK), lambda bi, i: (bi * (s // tq) + i, 0, 0, 0)),
            pl.BlockSpec((1, s, kv_width), lambda bi, i: (bi, 0, 0)),
            pl.BlockSpec((kv_width, s), lambda bi, i: (0, bi)),
            slab_a,
            slab_b,
        ],
        out_specs=[pl.BlockSpec((1, tq, attn_width), lambda bi, i: (bi, i, 0)), slab_a, slab_b],
        out_shape=[jax.ShapeDtypeStruct((b, s, attn_width), BF16),
                   jax.ShapeDtypeStruct(cast_a.shape, BF16),
                   jax.ShapeDtypeStruct(cast_b.shape, BF16)],
        scratch_shapes=[
            pltpu.VMEM((3, n_kv, KWIN, Q_PER_KV * QBLK), F32),
            pltpu.VMEM((n_kv, 1, Q_PER_KV * QBLK), F32),
            pltpu.VMEM((2, n_kv, KWIN, Q_PER_KV * QBLK), F32),
        ],
        compiler_params=pltpu.CompilerParams(
            dimension_semantics=("arbitrary", "arbitrary"), vmem_limit_bytes=V7X_VMEM_LIMIT),
        name="window_attn",
    )(sink, qt, k3, vt, cast_a, cast_b)


def _outproj_kernel(yp_ref, ya_ref, w_ref, x_ref, g_ref, o_ref, *, pool_width, tm):
    half = tm // 2
    for r0 in (0, half):
        rows = slice(r0, r0 + half)
        mix = jnp.dot(yp_ref[rows, :], w_ref[:pool_width, :], preferred_element_type=F32)
        mix = mix + jnp.dot(ya_ref[rows, :], w_ref[pool_width:, :], preferred_element_type=F32)
        o_ref[rows, :] = x_ref[rows, :] + _rms(mix, g_ref[...])


def _out_proj(yp, ya, w_out, x2, g, tm):
    n, d = x2.shape
    pw = yp.shape[1]
    aw = ya.shape[1]
    return pl.pallas_call(
        functools.partial(_outproj_kernel, pool_width=pw, tm=tm),
        grid=(n // tm,),
        in_specs=[
            pl.BlockSpec((tm, pw), lambda i: (i, 0)),
            pl.BlockSpec((tm, aw), lambda i: (i, 0)),
            pl.BlockSpec((pw + aw, d), lambda i: (0, 0), pipeline_mode=pl.Buffered(1)),
            pl.BlockSpec((tm, d), lambda i: (i, 0)),
            pl.BlockSpec((1, d), lambda i: (0, 0)),
        ],
        out_specs=pl.BlockSpec((tm, d), lambda i: (i, 0)),
        out_shape=jax.ShapeDtypeStruct((n, d), F32),
        compiler_params=pltpu.CompilerParams(
            dimension_semantics=("parallel",), vmem_limit_bytes=V7X_VMEM_LIMIT),
        name="out_proj",
    )(yp, ya, w_out, x2, g)


def _gelu_tanh(x):
    return 0.5 * x * (1.0 + jnp.tanh(np.sqrt(2.0 / np.pi).astype(np.float32) * (x + 0.044715 * (x * x * x))))


def _gate_up_kernel(x_ref, xp_ref, xn_ref, gpre_ref, wg_ref, wu_ref, cw_ref, cb_ref, h_ref, f_ref,
                    *, tm, tiles_per_seq):
    i = pl.program_id(0)
    c = pl.program_id(1)
    hal = CONV_HALO

    @pl.when(c == 0)
    def _():
        it = i % tiles_per_seq
        gp = gpre_ref[...]
        f_ref[0:tm, :] = _rms(x_ref[...], gp).astype(BF16)
        edge = jnp.concatenate([_rms(xn_ref[...], gp), _rms(xp_ref[...], gp)], axis=0)
        row = lax.broadcasted_iota(jnp.int32, (hal, 1), 0)
        next_row = jnp.where(it < tiles_per_seq - 1, 0, -1)
        prev_row = jnp.where(it > 0, hal - 1, -1)
        f_ref[tm:, :] = jnp.where((row == next_row) | (row == prev_row), edge, 0.0).astype(BF16)

    g = jnp.dot(f_ref[...], wg_ref[...], preferred_element_type=F32)
    up = jnp.dot(f_ref[0:tm, :], wu_ref[...], preferred_element_type=F32)
    tc = h_ref.shape[1]
    cols = pl.ds(pl.multiple_of(c * tc, tc), tc)
    cw = cw_ref[:, cols]
    g_prev = pltpu.roll(g, 1, axis=0)[0:tm, :]
    g_next = pltpu.roll(g, tm + hal - 1, axis=0)[0:tm, :]
    gate = g_prev * cw[0:1, :] + g[0:tm, :] * cw[1:2, :] + g_next * cw[2:3, :] + cb_ref[:, cols]
    h_ref[...] = (_gelu_tanh(gate) * up).astype(BF16)


def _gate_up(x1, g_pre, w_gate, w_up, conv_w, conv_b, seq, tm, tc):
    n, d = x1.shape
    dff = w_gate.shape[1]
    hal = CONV_HALO
    nb8 = tm // 8
    tiles_per_seq = seq // tm
    return pl.pallas_call(
        functools.partial(_gate_up_kernel, tm=tm, tiles_per_seq=tiles_per_seq),
        grid=(n // tm, dff // tc),
        in_specs=[
            pl.BlockSpec((tm, d), lambda i, c: (i, 0)),
            pl.BlockSpec((8, d), lambda i, c: (jnp.maximum(i * nb8 - 1, 0), 0)),
            pl.BlockSpec((8, d), lambda i, c: (jnp.minimum((i + 1) * nb8, n // 8 - 1), 0)),
            pl.BlockSpec((1, d), lambda i, c: (0, 0)),
            pl.BlockSpec((d, tc), lambda i, c: (0, c)),
            pl.BlockSpec((d, tc), lambda i, c: (0, c)),
            pl.BlockSpec((3, dff), lambda i, c: (0, 0)),
            pl.BlockSpec((1, dff), lambda i, c: (0, 0)),
        ],
        out_specs=pl.BlockSpec((tm, tc), lambda i, c: (i, c)),
        out_shape=jax.ShapeDtypeStruct((n, dff), BF16),
        scratch_shapes=[pltpu.VMEM((tm + hal, d), BF16)],
        compiler_params=pltpu.CompilerParams(
            dimension_semantics=("parallel", "arbitrary"), vmem_limit_bytes=V7X_VMEM_LIMIT),
        name="ffn_gate_up",
    )(x1, x1, x1, g_pre, w_gate, w_up, conv_w, conv_b)


def _down_kernel(h_ref, w_ref, x_ref, g_ref, o_ref, *, tm):
    half = tm // 2
    for r0 in (0, half):
        rows = slice(r0, r0 + half)
        y = jnp.dot(h_ref[rows, :], w_ref[...], preferred_element_type=F32)
        o_ref[rows, :] = x_ref[rows, :] + _rms(y, g_ref[...])


def _down(h, w_down, x1, g_post, tm):
    n, d = x1.shape
    dff = h.shape[1]
    return pl.pallas_call(
        functools.partial(_down_kernel, tm=tm),
        grid=(n // tm,),
        in_specs=[
            pl.BlockSpec((tm, dff), lambda i: (i, 0)),
            pl.BlockSpec((dff, d), lambda i: (0, 0), pipeline_mode=pl.Buffered(1)),
            pl.BlockSpec((tm, d), lambda i: (i, 0)),
            pl.BlockSpec((1, d), lambda i: (0, 0)),
        ],
        out_specs=pl.BlockSpec((tm, d), lambda i: (i, 0)),
        out_shape=jax.ShapeDtypeStruct((n, d), F32),
        compiler_params=pltpu.CompilerParams(
            dimension_semantics=("parallel",), vmem_limit_bytes=V7X_VMEM_LIMIT),
        name="ffn_down",
    )(h, w_down, x1, g_post)


def kernel(x, norm_pre_mix, w_in, w_pool, pool_scale, attn_sink, w_out, norm_post_mix, norm_pre_ffn,
           w_gate, w_up, conv_w, conv_b, w_down, norm_post_ffn):
    b, s, d = x.shape
    depth = w_in.shape[0]
    pool_width = pool_scale.shape[1]
    n_heads = attn_sink.shape[1]
    attn_width = n_heads * HEAD_DIM
    x2 = x.reshape(b * s, d)
    for l in range(depth):
        y_pool, qt, k, vt, w_out_bf, w_gate_bf = _in_proj_pool(
            x2, norm_pre_mix[l][None, :], w_in[l].astype(BF16), w_pool[l].astype(BF16), pool_scale[l][None, :],
            w_out[l], w_gate[l], s, pool_width, attn_width, tm=ROW_TILE)
        y_attn, w_up_bf, w_down_bf = _attention(qt, k.reshape(b, s, -1), vt, attn_sink[l], w_up[l], w_down[l],
                                                tq=ROW_TILE)
        x2 = _out_proj(y_pool, y_attn.reshape(b * s, attn_width), w_out_bf, x2, norm_post_mix[l][None, :], tm=ROW_TILE)
        hmid = _gate_up(x2, norm_pre_ffn[l][None, :], w_gate_bf, w_up_bf, conv_w[l], conv_b[l][None, :], seq=s,
                        tm=ROW_TILE, tc=FFN_CHUNK)
        x2 = _down(hmid, w_down_bf, x2, norm_post_ffn[l][None, :], tm=DOWN_TILE)
    return x2.reshape(b, s, d)
```

```python
import functools

import jax
import jax.numpy as jnp
import numpy as np
from jax import lax
from jax.experimental import pallas as pl
from jax.experimental.pallas import tpu as pltpu

F32 = jnp.float32
BF16 = jnp.bfloat16

EPS = 1e-6
LOG2E = 1.4426950408889634
POOL_WINDOWS = (2, 4, 8, 16)
HEAD_DIM = 64
Q_PER_KV = 4
WINDOW = 128
QBLK = 128
KWIN = 3 * QBLK
SUM_ROWS = 16
IN_HALO = 16
CONV_HALO = 16
FFN_CHUNK = 512
RING_SLOTS = 3
ROW_TILE = 1024
DOWN_TILE = 512
V7X_VMEM_LIMIT = 62 * 1024 * 1024


def _rms(x, g):
    return x * lax.rsqrt(jnp.mean(x * x, axis=-1, keepdims=True) + EPS) * g


def _roll_rows(a, shift):
    return pltpu.roll(a, shift % a.shape[0], axis=0)


def _inproj_pool_kernel(x_ref, xp_ref, xn_ref, g_ref, w_ref, wp_ref, ps_ref, ca_ref, cb_ref,
                        yp_ref, qt_ref, k_ref, vt_ref, ca_out_ref, cb_out_ref, h_ref,
                        *, tm, seq, pool_width, attn_width):
    ca_out_ref[...] = ca_ref[...].astype(BF16)
    cb_out_ref[...] = cb_ref[...].astype(BF16)
    i = pl.program_id(0)
    tiles_per_seq = seq // tm
    it = i % tiles_per_seq
    hal = IN_HALO
    half = tm // 2
    rows = tm + 2 * hal
    gw = pool_width // len(POOL_WINDOWS)
    gp = g_ref[...]

    h_ref[0:hal, :] = jnp.where(it > 0, _rms(xp_ref[...], gp), 0.0).astype(BF16)
    h_ref[hal:hal + half, :] = _rms(x_ref[0:half, :], gp).astype(BF16)
    h_ref[hal + half:hal + tm, :] = _rms(x_ref[half:tm, :], gp).astype(BF16)
    h_ref[hal + tm:, :] = jnp.where(it < tiles_per_seq - 1, _rms(xn_ref[...], gp), 0.0).astype(BF16)

    w_u = w_ref[:, :pool_width]
    u = jnp.concatenate([jnp.dot(h_ref[0:hal + half, :], w_u, preferred_element_type=F32),
                         jnp.dot(h_ref[hal + half:rows, :], w_u, preferred_element_type=F32)], axis=0)
    w_q = w_ref[:, pool_width:pool_width + attn_width]
    w_kv = w_ref[:, pool_width + attn_width:]
    kv_width = k_ref.shape[1]
    grp_width = Q_PER_KV * HEAD_DIM
    for r0 in (0, half):
        hm = h_ref[hal + r0:hal + r0 + half, :]
        q = jnp.dot(hm, w_q, preferred_element_type=F32) * (HEAD_DIM ** -0.5 * LOG2E)
        for jb in range(half // QBLK):
            for hk in range(attn_width // grp_width):
                t = q[jb * QBLK:(jb + 1) * QBLK, hk * grp_width:(hk + 1) * grp_width].T
                qt = jnp.concatenate([t[g * HEAD_DIM:(g + 1) * HEAD_DIM, :] for g in range(Q_PER_KV)], axis=1)
                qt_ref[r0 // QBLK + jb, hk] = qt.astype(BF16)
        kv = jnp.dot(hm, w_kv, preferred_element_type=F32)
        k_ref[r0:r0 + half, :] = kv[:, :kv_width].astype(BF16)
        vt_ref[:, r0:r0 + half] = kv[:, kv_width:].T.astype(BF16)

    a1 = u + _roll_rows(u, 1)
    a1r = a1[:, gw:]
    a2 = _roll_rows(a1r, -1) + _roll_rows(a1r, 1)
    a2r = a2[:, gw:]
    a4 = _roll_rows(a2r, -2) + _roll_rows(a2r, 2)
    a4r = a4[:, gw:]
    a8 = _roll_rows(a4r, -4) + _roll_rows(a4r, 4)
    main = slice(hal, hal + tm)
    wins = (a1[main, :gw], a2[main, :gw], a4[main, :gw], a8[main, :])
    t = it * tm + lax.broadcasted_iota(jnp.int32, (tm, 1), 0)
    for g, w in enumerate(POOL_WINDOWS):
        lo = jnp.maximum(t - w // 2, 0)
        hi = jnp.minimum(t + w // 2, seq)
        inv_cnt = 1.0 / (hi - lo).astype(F32)
        cols = slice(g * gw, (g + 1) * gw)
        d = (wins[g] * inv_cnt - u[main, cols]).astype(BF16)
        y = jnp.dot(d, wp_ref[g], preferred_element_type=F32) * ps_ref[:, cols]
        yp_ref[:, cols] = y.astype(BF16)


def _in_proj_pool(x2, g, w_in, w_pool, pool_scale, cast_a, cast_b, seq, pool_width, attn_width, tm):
    n, d = x2.shape
    in_width = w_in.shape[1]
    ng = len(POOL_WINDOWS)
    gw = pool_width // ng
    hal = IN_HALO
    nh = tm // hal
    steps = n // tm
    kv_width = (in_width - pool_width - attn_width) // 2
    n_kv = attn_width // (Q_PER_KV * HEAD_DIM)
    resident = dict(pipeline_mode=pl.Buffered(1))
    slab_a = pl.BlockSpec((cast_a.shape[0] // steps, cast_a.shape[1]), lambda i: (i, 0))
    slab_b = pl.BlockSpec((cast_b.shape[0] // steps, cast_b.shape[1]), lambda i: (i, 0))
    return pl.pallas_call(
        functools.partial(_inproj_pool_kernel, tm=tm, seq=seq, pool_width=pool_width, attn_width=attn_width),
        grid=(n // tm,),
        in_specs=[
            pl.BlockSpec((tm, d), lambda i: (i, 0)),
            pl.BlockSpec((hal, d), lambda i: (jnp.maximum(i * nh - 1, 0), 0)),
            pl.BlockSpec((hal, d), lambda i: (jnp.minimum((i + 1) * nh, n // hal - 1), 0)),
            pl.BlockSpec((1, d), lambda i: (0, 0)),
            pl.BlockSpec((d, in_width), lambda i: (0, 0), **resident),
            pl.BlockSpec((ng, gw, gw), lambda i: (0, 0, 0), **resident),
            pl.BlockSpec((1, pool_width), lambda i: (0, 0)),
            slab_a,
            slab_b,
        ],
        out_specs=[
            pl.BlockSpec((tm, pool_width), lambda i: (i, 0)),
            pl.BlockSpec((tm // QBLK, n_kv, HEAD_DIM, Q_PER_KV * QBLK), lambda i: (i, 0, 0, 0)),
            pl.BlockSpec((tm, kv_width), lambda i: (i, 0)),
            pl.BlockSpec((kv_width, tm), lambda i: (0, i)),
            slab_a,
            slab_b,
        ],
        out_shape=[
            jax.ShapeDtypeStruct((n, pool_width), BF16),
            jax.ShapeDtypeStruct((n // QBLK, n_kv, HEAD_DIM, Q_PER_KV * QBLK), BF16),
            jax.ShapeDtypeStruct((n, kv_width), BF16),
            jax.ShapeDtypeStruct((kv_width, n), BF16),
            jax.ShapeDtypeStruct(cast_a.shape, BF16),
            jax.ShapeDtypeStruct(cast_b.shape, BF16),
        ],
        scratch_shapes=[pltpu.VMEM((tm + 2 * hal, d), BF16)],
        compiler_params=pltpu.CompilerParams(
            dimension_semantics=("parallel",), vmem_limit_bytes=V7X_VMEM_LIMIT),
        name="in_proj_pool",
    )(x2, x2, x2, g, w_in, w_pool, pool_scale, cast_a, cast_b)


def _attn_kernel(sink_ref, qt_ref, k_ref, vt_ref, ca_ref, cb_ref, o_ref, ca_out_ref, cb_out_ref,
                 bias_ref, sinkrow_ref, st_ref, *, tq, seq, slopes):
    ca_out_ref[...] = ca_ref[...].astype(BF16)
    cb_out_ref[...] = cb_ref[...].astype(BF16)
    bi = pl.program_id(0)
    i = pl.program_id(1)
    grp = Q_PER_KV
    n_heads = len(slopes)
    n_kv = n_heads // grp

    @pl.when((bi == 0) & (i == 0))
    def _():
        kj = lax.broadcasted_iota(jnp.int32, (KWIN, QBLK), 0)
        qi = lax.broadcasted_iota(jnp.int32, (KWIN, QBLK), 1)
        for v in range(3):
            dist_i = jnp.abs(v * QBLK + qi - kj)
            valid = dist_i <= WINDOW
            dist = dist_i.astype(F32)
            for h in range(n_heads):
                cols = slice((h % grp) * QBLK, (h % grp + 1) * QBLK)
                bias_ref[v, h // grp, :, cols] = jnp.where(valid, -(slopes[h] * dist) * LOG2E, -jnp.inf)
        for h in range(n_heads):
            cols = slice((h % grp) * QBLK, (h % grp + 1) * QBLK)
            sinkrow_ref[h // grp, :, cols] = jnp.full((1, QBLK), sink_ref[h] * LOG2E, F32)

    def sub_block(j, carry):
        q0 = i * tq + j * QBLK
        k0 = pl.multiple_of(jnp.clip(q0 - WINDOW, 0, seq - KWIN), QBLK)
        variant = (q0 - k0) // QBLK
        rows = pl.ds(pl.multiple_of(j * QBLK, QBLK), QBLK)
        par = (i + j) % 2
        m_all = []
        for hk in range(n_kv):
            k = k_ref[0, pl.ds(k0, KWIN), hk * HEAD_DIM:(hk + 1) * HEAD_DIM]
            st = jnp.dot(k, qt_ref[j, hk], preferred_element_type=F32)
            st_ref[par, hk] = st + bias_ref[variant, hk]
            m_all.append(jnp.maximum(jnp.max(st_ref[par, hk], axis=0, keepdims=True), sinkrow_ref[hk]))
        ones = jnp.ones((SUM_ROWS, KWIN), BF16)
        for hk in range(n_kv):
            e = jnp.exp2(st_ref[par, hk] - m_all[hk]).astype(BF16)
            vt = jnp.concatenate([vt_ref[hk * HEAD_DIM:(hk + 1) * HEAD_DIM, pl.ds(k0, KWIN)], ones], axis=0)
            ot = jnp.dot(vt, e, preferred_element_type=F32)
            den = ot[HEAD_DIM:HEAD_DIM + 1, :] + jnp.exp2(sinkrow_ref[hk] - m_all[hk])
            ot = ot[:HEAD_DIM, :] * (1.0 / den)
            o4 = jnp.concatenate([ot[:, g * QBLK:(g + 1) * QBLK].T for g in range(grp)], axis=1)
            o_ref[0, rows, hk * grp * HEAD_DIM:(hk + 1) * grp * HEAD_DIM] = o4.astype(BF16)
        return carry

    for j in range(tq // QBLK):
        sub_block(j, 0)


def _attention(qt, k3, vt, sink, cast_a, cast_b, tq):
    b, s, kv_width = k3.shape
    n_kv = qt.shape[1]
    attn_width = n_kv * Q_PER_KV * HEAD_DIM
    steps = b * (s // tq)
    slab_a = pl.BlockSpec((cast_a.shape[0] // steps, cast_a.shape[1]), lambda bi, i: (bi * (s // tq) + i, 0))
    slab_b = pl.BlockSpec((cast_b.shape[0] // steps, cast_b.shape[1]), lambda bi, i: (bi * (s // tq) + i, 0))
    n_heads = attn_width // HEAD_DIM
    idx = np.arange(1, n_heads + 1, dtype=np.float32)
    slopes = tuple(float(v) for v in np.power(2.0, -8.0 * idx / n_heads).astype(np.float32))
    return pl.pallas_call(
        functools.partial(_attn_kernel, tq=tq, seq=s, slopes=slopes),
        grid=(b, s // tq),
        in_specs=[
            pl.BlockSpec(memory_space=pltpu.SMEM),
            pl.BlockSpec((tq // QBLK, n_kv, HEAD_DIM, Q_PER_KV * QBLK), lambda bi, i: (bi * (s // tq) + i, 0, 0, 0)),
            pl.BlockSpec((1, s, kv_width), lambda bi, i: (bi, 0, 0)),
            pl.BlockSpec((kv_width, s), lambda bi, i: (0, bi)),
            slab_a,
            slab_b,
        ],
        out_specs=[pl.BlockSpec((1, tq, attn_width), lambda bi, i: (bi, i, 0)), slab_a, slab_b],
        out_shape=[jax.ShapeDtypeStruct((b, s, attn_width), BF16),
                   jax.ShapeDtypeStruct(cast_a.shape, BF16),
                   jax.ShapeDtypeStruct(cast_b.shape, BF16)],
        scratch_shapes=[
            pltpu.VMEM((3, n_kv, KWIN, Q_PER_KV * QBLK), F32),
            pltpu.VMEM((n_kv, 1, Q_PER_KV * QBLK), F32),
            pltpu.VMEM((2, n_kv, KWIN, Q_PER_KV * QBLK), F32),
        ],
        compiler_params=pltpu.CompilerParams(
            dimension_semantics=("arbitrary", "arbitrary"), vmem_limit_bytes=V7X_VMEM_LIMIT),
        name="window_attn",
    )(sink, qt, k3, vt, cast_a, cast_b)


def _outproj_kernel(yp_ref, ya_ref, w_ref, x_ref, g_ref, o_ref, *, pool_width, tm):
    half = tm // 2
    for r0 in (0, half):
        rows = slice(r0, r0 + half)
        mix = jnp.dot(yp_ref[rows, :], w_ref[:pool_width, :], preferred_element_type=F32)
        mix = mix + jnp.dot(ya_ref[rows, :], w_ref[pool_width:, :], preferred_element_type=F32)
        o_ref[rows, :] = x_ref[rows, :] + _rms(mix, g_ref[...])


def _out_proj(yp, ya, w_out, x2, g, tm):
    n, d = x2.shape
    pw = yp.shape[1]
    aw = ya.shape[1]
    return pl.pallas_call(
        functools.partial(_outproj_kernel, pool_width=pw, tm=tm),
        grid=(n // tm,),
        in_specs=[
            pl.BlockSpec((tm, pw), lambda i: (i, 0)),
            pl.BlockSpec((tm, aw), lambda i: (i, 0)),
            pl.BlockSpec((pw + aw, d), lambda i: (0, 0), pipeline_mode=pl.Buffered(1)),
            pl.BlockSpec((tm, d), lambda i: (i, 0)),
            pl.BlockSpec((1, d), lambda i: (0, 0)),
        ],
        out_specs=pl.BlockSpec((tm, d), lambda i: (i, 0)),
        out_shape=jax.ShapeDtypeStruct((n, d), F32),
        compiler_params=pltpu.CompilerParams(
            dimension_semantics=("parallel",), vmem_limit_bytes=V7X_VMEM_LIMIT),
        name="out_proj",
    )(yp, ya, w_out, x2, g)


def _gelu_tanh(x):
    return 0.5 * x * (1.0 + jnp.tanh(np.sqrt(2.0 / np.pi).astype(np.float32) * (x + 0.044715 * (x * x * x))))


def _gate_up_kernel(x_ref, xp_ref, xn_ref, gpre_ref, cw_ref, cb_ref, wg_hbm, wu_hbm, h_ref, f_ref, w_buf, sem,
                    *, tm, tc, n_chunks, n_tiles, tiles_per_seq):
    i = pl.program_id(0)
    hal = CONV_HALO

    def chunk_copies(c, slot):
        cols = pl.ds(pl.multiple_of(c * tc, tc), tc)
        return (pltpu.make_async_copy(wg_hbm.at[:, cols], w_buf.at[slot, 0], sem.at[slot, 0]),
                pltpu.make_async_copy(wu_hbm.at[:, cols], w_buf.at[slot, 1], sem.at[slot, 1]))

    @pl.when(i == 0)
    def _():
        for c0 in range(RING_SLOTS - 1):
            for cp in chunk_copies(c0, c0):
                cp.start()

    it = i % tiles_per_seq
    gp = gpre_ref[...]
    f_ref[0:tm, :] = _rms(x_ref[...], gp).astype(BF16)
    edge = jnp.concatenate([_rms(xn_ref[...], gp), _rms(xp_ref[...], gp)], axis=0)
    row = lax.broadcasted_iota(jnp.int32, (hal, 1), 0)
    next_row = jnp.where(it < tiles_per_seq - 1, 0, -1)
    prev_row = jnp.where(it > 0, hal - 1, -1)
    f_ref[tm:, :] = jnp.where((row == next_row) | (row == prev_row), edge, 0.0).astype(BF16)

    def chunk(c, carry):
        t = i * n_chunks + c
        slot = t % RING_SLOTS
        ahead = RING_SLOTS - 1

        @pl.when(t + ahead < n_tiles * n_chunks)
        def _():
            c_next = jnp.where(c + ahead >= n_chunks, c + ahead - n_chunks, c + ahead)
            for cp in chunk_copies(c_next, (t + ahead) % RING_SLOTS):
                cp.start()

        for cp in chunk_copies(c, slot):
            cp.wait()
        g = jnp.dot(f_ref[...], w_buf[slot, 0], preferred_element_type=F32)
        up = jnp.dot(f_ref[0:tm, :], w_buf[slot, 1], preferred_element_type=F32)
        cols = pl.ds(pl.multiple_of(c * tc, tc), tc)
        cw = cw_ref[:, cols]
        g_prev = pltpu.roll(g, 1, axis=0)[0:tm, :]
        g_next = pltpu.roll(g, tm + hal - 1, axis=0)[0:tm, :]
        gate = g_prev * cw[0:1, :] + g[0:tm, :] * cw[1:2, :] + g_next * cw[2:3, :] + cb_ref[:, cols]
        h_ref[:, cols] = (_gelu_tanh(gate) * up).astype(BF16)
        return carry

    lax.fori_loop(0, n_chunks, chunk, 0)


def _gate_up(x1, g_pre, w_gate, w_up, conv_w, conv_b, seq, tm, tc):
    n, d = x1.shape
    dff = w_gate.shape[1]
    hal = CONV_HALO
    nb8 = tm // 8
    tiles_per_seq = seq // tm
    return pl.pallas_call(
        functools.partial(_gate_up_kernel, tm=tm, tc=tc, n_chunks=dff // tc, n_tiles=n // tm,
                          tiles_per_seq=tiles_per_seq),
        grid=(n // tm,),
        in_specs=[
            pl.BlockSpec((tm, d), lambda i: (i, 0)),
            pl.BlockSpec((8, d), lambda i: (jnp.maximum(i * nb8 - 1, 0), 0)),
            pl.BlockSpec((8, d), lambda i: (jnp.minimum((i + 1) * nb8, n // 8 - 1), 0)),
            pl.BlockSpec((1, d), lambda i: (0, 0)),
            pl.BlockSpec((3, dff), lambda i: (0, 0)),
            pl.BlockSpec((1, dff), lambda i: (0, 0)),
            pl.BlockSpec(memory_space=pl.ANY),
            pl.BlockSpec(memory_space=pl.ANY),
        ],
        out_specs=pl.BlockSpec((tm, dff), lambda i: (i, 0)),
        out_shape=jax.ShapeDtypeStruct((n, dff), BF16),
        scratch_shapes=[pltpu.VMEM((tm + hal, d), BF16), pltpu.VMEM((RING_SLOTS, 2, d, tc), BF16),
                        pltpu.SemaphoreType.DMA((RING_SLOTS, 2))],
        compiler_params=pltpu.CompilerParams(
            dimension_semantics=("arbitrary",), vmem_limit_bytes=V7X_VMEM_LIMIT),
        name="ffn_gate_up",
    )(x1, x1, x1, g_pre, conv_w, conv_b, w_gate, w_up)


def _down_kernel(h_ref, w_ref, x_ref, g_ref, o_ref, *, tm):
    half = tm // 2
    for r0 in (0, half):
        rows = slice(r0, r0 + half)
        y = jnp.dot(h_ref[rows, :], w_ref[...], preferred_element_type=F32)
        o_ref[rows, :] = x_ref[rows, :] + _rms(y, g_ref[...])


def _down(h, w_down, x1, g_post, tm):
    n, d = x1.shape
    dff = h.shape[1]
    return pl.pallas_call(
        functools.partial(_down_kernel, tm=tm),
        grid=(n // tm,),
        in_specs=[
            pl.BlockSpec((tm, dff), lambda i: (i, 0)),
            pl.BlockSpec((dff, d), lambda i: (0, 0), pipeline_mode=pl.Buffered(1)),
            pl.BlockSpec((tm, d), lambda i: (i, 0)),
            pl.BlockSpec((1, d), lambda i: (0, 0)),
        ],
        out_specs=pl.BlockSpec((tm, d), lambda i: (i, 0)),
        out_shape=jax.ShapeDtypeStruct((n, d), F32),
        compiler_params=pltpu.CompilerParams(
            dimension_semantics=("parallel",), vmem_limit_bytes=V7X_VMEM_LIMIT),
        name="ffn_down",
    )(h, w_down, x1, g_post)


def kernel(x, norm_pre_mix, w_in, w_pool, pool_scale, attn_sink, w_out, norm_post_mix, norm_pre_ffn,
           w_gate, w_up, conv_w, conv_b, w_down, norm_post_ffn):
    b, s, d = x.shape
    depth = w_in.shape[0]
    pool_width = pool_scale.shape[1]
    n_heads = attn_sink.shape[1]
    attn_width = n_heads * HEAD_DIM
    x2 = x.reshape(b * s, d)
    for l in range(depth):
        y_pool, qt, k, vt, w_out_bf, w_gate_bf = _in_proj_pool(
            x2, norm_pre_mix[l][None, :], w_in[l].astype(BF16), w_pool[l].astype(BF16), pool_scale[l][None, :],
            w_out[l], w_gate[l], s, pool_width, attn_width, tm=ROW_TILE)
        y_attn, w_up_bf, w_down_bf = _attention(qt, k.reshape(b, s, -1), vt, attn_sink[l], w_up[l], w_down[l],
                                                tq=ROW_TILE)
        x2 = _out_proj(y_pool, y_attn.reshape(b * s, attn_width), w_out_bf, x2, norm_post_mix[l][None, :], tm=ROW_TILE)
        hmid = _gate_up(x2, norm_pre_ffn[l][None, :], w_gate_bf, w_up_bf, conv_w[l], conv_b[l][None, :], seq=s,
                        tm=ROW_TILE, tc=FFN_CHUNK)
        x2 = _down(hmid, w_down_bf, x2, norm_post_ffn[l][None, :], tm=DOWN_TILE)
    return x2.reshape(b, s, d)
```

```python
import functools

import jax
import jax.numpy as jnp
import numpy as np
from jax import lax
from jax.experimental import pallas as pl
from jax.experimental.pallas import tpu as pltpu

F32 = jnp.float32
BF16 = jnp.bfloat16

EPS = 1e-6
LOG2E = 1.4426950408889634
POOL_WINDOWS = (2, 4, 8, 16)
HEAD_DIM = 64
Q_PER_KV = 4
WINDOW = 128
QBLK = 128
KWIN = 3 * QBLK
SUM_ROWS = 16
IN_HALO = 16
CONV_HALO = 16
FFN_CHUNK = 512
RING_SLOTS = 4
DOWN_SLOTS = 3
ROW_TILE = 1024
DOWN_TILE = 512
V7X_VMEM_LIMIT = 62 * 1024 * 1024


def _rms(x, g):
    return x * lax.rsqrt(jnp.mean(x * x, axis=-1, keepdims=True) + EPS) * g


def _roll_rows(a, shift):
    return pltpu.roll(a, shift % a.shape[0], axis=0)


def _inproj_pool_kernel(x_ref, xp_ref, xn_ref, g_ref, w_ref, wp_ref, ps_ref, ca_ref, cb_ref,
                        yp_ref, qt_ref, k_ref, vt_ref, ca_out_ref, cb_out_ref, h_ref,
                        *, tm, seq, pool_width, attn_width):
    ca_out_ref[...] = ca_ref[...].astype(BF16)
    cb_out_ref[...] = cb_ref[...].astype(BF16)
    i = pl.program_id(0)
    tiles_per_seq = seq // tm
    it = i % tiles_per_seq
    hal = IN_HALO
    half = tm // 2
    rows = tm + 2 * hal
    gw = pool_width // len(POOL_WINDOWS)
    gp = g_ref[...]

    h_ref[0:hal, :] = jnp.where(it > 0, _rms(xp_ref[...], gp), 0.0).astype(BF16)
    h_ref[hal:hal + half, :] = _rms(x_ref[0:half, :], gp).astype(BF16)
    h_ref[hal + half:hal + tm, :] = _rms(x_ref[half:tm, :], gp).astype(BF16)
    h_ref[hal + tm:, :] = jnp.where(it < tiles_per_seq - 1, _rms(xn_ref[...], gp), 0.0).astype(BF16)

    w_u = w_ref[:, :pool_width]
    u = jnp.concatenate([jnp.dot(h_ref[0:hal + half, :], w_u, preferred_element_type=F32),
                         jnp.dot(h_ref[hal + half:rows, :], w_u, preferred_element_type=F32)], axis=0)
    w_q = w_ref[:, pool_width:pool_width + attn_width]
    w_kv = w_ref[:, pool_width + attn_width:]
    kv_width = k_ref.shape[1]
    grp_width = Q_PER_KV * HEAD_DIM
    for r0 in (0, half):
        hm = h_ref[hal + r0:hal + r0 + half, :]
        q = jnp.dot(hm, w_q, preferred_element_type=F32) * (HEAD_DIM ** -0.5 * LOG2E)
        for jb in range(half // QBLK):
            for hk in range(attn_width // grp_width):
                t = q[jb * QBLK:(jb + 1) * QBLK, hk * grp_width:(hk + 1) * grp_width].T
                qt = jnp.concatenate([t[g * HEAD_DIM:(g + 1) * HEAD_DIM, :] for g in range(Q_PER_KV)], axis=1)
                qt_ref[r0 // QBLK + jb, hk] = qt.astype(BF16)
        kv = jnp.dot(hm, w_kv, preferred_element_type=F32)
        k_ref[r0:r0 + half, :] = kv[:, :kv_width].astype(BF16)
        vt_ref[:, r0:r0 + half] = kv[:, kv_width:].T.astype(BF16)

    a1 = u + _roll_rows(u, 1)
    a1r = a1[:, gw:]
    a2 = _roll_rows(a1r, -1) + _roll_rows(a1r, 1)
    a2r = a2[:, gw:]
    a4 = _roll_rows(a2r, -2) + _roll_rows(a2r, 2)
    a4r = a4[:, gw:]
    a8 = _roll_rows(a4r, -4) + _roll_rows(a4r, 4)
    main = slice(hal, hal + tm)
    wins = (a1[main, :gw], a2[main, :gw], a4[main, :gw], a8[main, :])
    t = it * tm + lax.broadcasted_iota(jnp.int32, (tm, 1), 0)
    for g, w in enumerate(POOL_WINDOWS):
        lo = jnp.maximum(t - w // 2, 0)
        hi = jnp.minimum(t + w // 2, seq)
        inv_cnt = 1.0 / (hi - lo).astype(F32)
        cols = slice(g * gw, (g + 1) * gw)
        d = (wins[g] * inv_cnt - u[main, cols]).astype(BF16)
        y = jnp.dot(d, wp_ref[g], preferred_element_type=F32) * ps_ref[:, cols]
        yp_ref[:, cols] = y.astype(BF16)


def _in_proj_pool(x2, g, w_in, w_pool, pool_scale, cast_a, cast_b, seq, pool_width, attn_width, tm):
    n, d = x2.shape
    in_width = w_in.shape[1]
    ng = len(POOL_WINDOWS)
    gw = pool_width // ng
    hal = IN_HALO
    nh = tm // hal
    steps = n // tm
    kv_width = (in_width - pool_width - attn_width) // 2
    n_kv = attn_width // (Q_PER_KV * HEAD_DIM)
    resident = dict(pipeline_mode=pl.Buffered(1))
    slab_a = pl.BlockSpec((cast_a.shape[0] // steps, cast_a.shape[1]), lambda i: (i, 0))
    slab_b = pl.BlockSpec((cast_b.shape[0] // steps, cast_b.shape[1]), lambda i: (i, 0))
    return pl.pallas_call(
        functools.partial(_inproj_pool_kernel, tm=tm, seq=seq, pool_width=pool_width, attn_width=attn_width),
        grid=(n // tm,),
        in_specs=[
            pl.BlockSpec((tm, d), lambda i: (i, 0)),
            pl.BlockSpec((hal, d), lambda i: (jnp.maximum(i * nh - 1, 0), 0)),
            pl.BlockSpec((hal, d), lambda i: (jnp.minimum((i + 1) * nh, n // hal - 1), 0)),
            pl.BlockSpec((1, d), lambda i: (0, 0)),
            pl.BlockSpec((d, in_width), lambda i: (0, 0), **resident),
            pl.BlockSpec((ng, gw, gw), lambda i: (0, 0, 0), **resident),
            pl.BlockSpec((1, pool_width), lambda i: (0, 0)),
            slab_a,
            slab_b,
        ],
        out_specs=[
            pl.BlockSpec((tm, pool_width), lambda i: (i, 0)),
            pl.BlockSpec((tm // QBLK, n_kv, HEAD_DIM, Q_PER_KV * QBLK), lambda i: (i, 0, 0, 0)),
            pl.BlockSpec((tm, kv_width), lambda i: (i, 0)),
            pl.BlockSpec((kv_width, tm), lambda i: (0, i)),
            slab_a,
            slab_b,
        ],
        out_shape=[
            jax.ShapeDtypeStruct((n, pool_width), BF16),
            jax.ShapeDtypeStruct((n // QBLK, n_kv, HEAD_DIM, Q_PER_KV * QBLK), BF16),
            jax.ShapeDtypeStruct((n, kv_width), BF16),
            jax.ShapeDtypeStruct((kv_width, n), BF16),
            jax.ShapeDtypeStruct(cast_a.shape, BF16),
            jax.ShapeDtypeStruct(cast_b.shape, BF16),
        ],
        scratch_shapes=[pltpu.VMEM((tm + 2 * hal, d), BF16)],
        compiler_params=pltpu.CompilerParams(
            dimension_semantics=("parallel",), vmem_limit_bytes=V7X_VMEM_LIMIT),
        name="in_proj_pool",
    )(x2, x2, x2, g, w_in, w_pool, pool_scale, cast_a, cast_b)


def _attn_kernel(sink_ref, qt_ref, k_ref, vt_ref, ca_ref, cb_ref, o_ref, ca_out_ref, cb_out_ref,
                 bias_ref, sinkrow_ref, st_ref, *, tq, seq, slopes):
    ca_out_ref[...] = ca_ref[...].astype(BF16)
    cb_out_ref[...] = cb_ref[...].astype(BF16)
    bi = pl.program_id(0)
    i = pl.program_id(1)
    grp = Q_PER_KV
    n_heads = len(slopes)
    n_kv = n_heads // grp

    @pl.when((bi == 0) & (i == 0))
    def _():
        kj = lax.broadcasted_iota(jnp.int32, (KWIN, QBLK), 0)
        qi = lax.broadcasted_iota(jnp.int32, (KWIN, QBLK), 1)
        for v in range(3):
            dist_i = jnp.abs(v * QBLK + qi - kj)
            valid = dist_i <= WINDOW
            dist = dist_i.astype(F32)
            for h in range(n_heads):
                cols = slice((h % grp) * QBLK, (h % grp + 1) * QBLK)
                bias_ref[v, h // grp, :, cols] = jnp.where(valid, -(slopes[h] * dist) * LOG2E, -jnp.inf)
        for h in range(n_heads):
            cols = slice((h % grp) * QBLK, (h % grp + 1) * QBLK)
            sinkrow_ref[h // grp, :, cols] = jnp.full((1, QBLK), sink_ref[h] * LOG2E, F32)

    def sub_block(j, carry):
        q0 = i * tq + j * QBLK
        k0 = pl.multiple_of(jnp.clip(q0 - WINDOW, 0, seq - KWIN), QBLK)
        variant = (q0 - k0) // QBLK
        rows = pl.ds(pl.multiple_of(j * QBLK, QBLK), QBLK)
        par = (i + j) % 2
        m_all = []
        for hk in range(n_kv):
            k = k_ref[0, pl.ds(k0, KWIN), hk * HEAD_DIM:(hk + 1) * HEAD_DIM]
            st = jnp.dot(k, qt_ref[j, hk], preferred_element_type=F32)
            st_ref[par, hk] = st + bias_ref[variant, hk]
            m_all.append(jnp.maximum(jnp.max(st_ref[par, hk], axis=0, keepdims=True), sinkrow_ref[hk]))
        ones = jnp.ones((SUM_ROWS, KWIN), BF16)
        for hk in range(n_kv):
            e = jnp.exp2(st_ref[par, hk] - m_all[hk]).astype(BF16)
            vt = jnp.concatenate([vt_ref[hk * HEAD_DIM:(hk + 1) * HEAD_DIM, pl.ds(k0, KWIN)], ones], axis=0)
            ot = jnp.dot(vt, e, preferred_element_type=F32)
            den = ot[HEAD_DIM:HEAD_DIM + 1, :] + jnp.exp2(sinkrow_ref[hk] - m_all[hk])
            ot = ot[:HEAD_DIM, :] * (1.0 / den)
            o4 = jnp.concatenate([ot[:, g * QBLK:(g + 1) * QBLK].T for g in range(grp)], axis=1)
            o_ref[0, rows, hk * grp * HEAD_DIM:(hk + 1) * grp * HEAD_DIM] = o4.astype(BF16)
        return carry

    for j in range(tq // QBLK):
        sub_block(j, 0)


def _attention(qt, k3, vt, sink, cast_a, cast_b, tq):
    b, s, kv_width = k3.shape
    n_kv = qt.shape[1]
    attn_width = n_kv * Q_PER_KV * HEAD_DIM
    steps = b * (s // tq)
    slab_a = pl.BlockSpec((cast_a.shape[0] // steps, cast_a.shape[1]), lambda bi, i: (bi * (s // tq) + i, 0))
    slab_b = pl.BlockSpec((cast_b.shape[0] // steps, cast_b.shape[1]), lambda bi, i: (bi * (s // tq) + i, 0))
    n_heads = attn_width // HEAD_DIM
    idx = np.arange(1, n_heads + 1, dtype=np.float32)
    slopes = tuple(float(v) for v in np.power(2.0, -8.0 * idx / n_heads).astype(np.float32))
    return pl.pallas_call(
        functools.partial(_attn_kernel, tq=tq, seq=s, slopes=slopes),
        grid=(b, s // tq),
        in_specs=[
            pl.BlockSpec(memory_space=pltpu.SMEM),
            pl.BlockSpec((tq // QBLK, n_kv, HEAD_DIM, Q_PER_KV * QBLK), lambda bi, i: (bi * (s // tq) + i, 0, 0, 0)),
            pl.BlockSpec((1, s, kv_width), lambda bi, i: (bi, 0, 0)),
            pl.BlockSpec((kv_width, s), lambda bi, i: (0, bi)),
            slab_a,
            slab_b,
        ],
        out_specs=[pl.BlockSpec((1, tq, attn_width), lambda bi, i: (bi, i, 0)), slab_a, slab_b],
        out_shape=[jax.ShapeDtypeStruct((b, s, attn_width), BF16),
                   jax.ShapeDtypeStruct(cast_a.shape, BF16),
                   jax.ShapeDtypeStruct(cast_b.shape, BF16)],
        scratch_shapes=[
            pltpu.VMEM((3, n_kv, KWIN, Q_PER_KV * QBLK), F32),
            pltpu.VMEM((n_kv, 1, Q_PER_KV * QBLK), F32),
            pltpu.VMEM((2, n_kv, KWIN, Q_PER_KV * QBLK), F32),
        ],
        compiler_params=pltpu.CompilerParams(
            dimension_semantics=("arbitrary", "arbitrary"), vmem_limit_bytes=V7X_VMEM_LIMIT),
        name="window_attn",
    )(sink, qt, k3, vt, cast_a, cast_b)


def _outproj_kernel(yp_ref, ya_ref, w_ref, x_ref, g_ref, o_ref, *, pool_width, tm):
    half = tm // 2
    for r0 in (0, half):
        rows = slice(r0, r0 + half)
        mix = jnp.dot(yp_ref[rows, :], w_ref[:pool_width, :], preferred_element_type=F32)
        mix = mix + jnp.dot(ya_ref[rows, :], w_ref[pool_width:, :], preferred_element_type=F32)
        o_ref[rows, :] = x_ref[rows, :] + _rms(mix, g_ref[...])


def _out_proj(yp, ya, w_out, x2, g, tm):
    n, d = x2.shape
    pw = yp.shape[1]
    aw = ya.shape[1]
    return pl.pallas_call(
        functools.partial(_outproj_kernel, pool_width=pw, tm=tm),
        grid=(n // tm,),
        in_specs=[
            pl.BlockSpec((tm, pw), lambda i: (i, 0)),
            pl.BlockSpec((tm, aw), lambda i: (i, 0)),
            pl.BlockSpec((pw + aw, d), lambda i: (0, 0), pipeline_mode=pl.Buffered(1)),
            pl.BlockSpec((tm, d), lambda i: (i, 0)),
            pl.BlockSpec((1, d), lambda i: (0, 0)),
        ],
        out_specs=pl.BlockSpec((tm, d), lambda i: (i, 0)),
        out_shape=jax.ShapeDtypeStruct((n, d), F32),
        compiler_params=pltpu.CompilerParams(
            dimension_semantics=("parallel",), vmem_limit_bytes=V7X_VMEM_LIMIT),
        name="out_proj",
    )(yp, ya, w_out, x2, g)


def _gelu_tanh(x):
    return 0.5 * x * (1.0 + jnp.tanh(np.sqrt(2.0 / np.pi).astype(np.float32) * (x + 0.044715 * (x * x * x))))


def _gate_up_kernel(x_ref, xp_ref, xn_ref, gpre_ref, cw_ref, cb_ref, wg_hbm, wu_hbm, h_ref, f_ref, w_buf, sem,
                    *, tm, tc, n_chunks, n_tiles, tiles_per_seq):
    i = pl.program_id(0)
    hal = CONV_HALO

    def chunk_copies(c, slot):
        cols = pl.ds(pl.multiple_of(c * tc, tc), tc)
        return (pltpu.make_async_copy(wg_hbm.at[:, cols], w_buf.at[slot, 0], sem.at[slot, 0]),
                pltpu.make_async_copy(wu_hbm.at[:, cols], w_buf.at[slot, 1], sem.at[slot, 1]))

    @pl.when(i == 0)
    def _():
        for c0 in range(RING_SLOTS - 1):
            for cp in chunk_copies(c0, c0):
                cp.start()

    it = i % tiles_per_seq
    gp = gpre_ref[...]
    f_ref[0:tm, :] = _rms(x_ref[...], gp).astype(BF16)
    edge = jnp.concatenate([_rms(xn_ref[...], gp), _rms(xp_ref[...], gp)], axis=0)
    row = lax.broadcasted_iota(jnp.int32, (hal, 1), 0)
    next_row = jnp.where(it < tiles_per_seq - 1, 0, -1)
    prev_row = jnp.where(it > 0, hal - 1, -1)
    f_ref[tm:, :] = jnp.where((row == next_row) | (row == prev_row), edge, 0.0).astype(BF16)

    def chunk(c, carry):
        t = i * n_chunks + c
        slot = t % RING_SLOTS
        ahead = RING_SLOTS - 1

        @pl.when(t + ahead < n_tiles * n_chunks)
        def _():
            c_next = jnp.where(c + ahead >= n_chunks, c + ahead - n_chunks, c + ahead)
            for cp in chunk_copies(c_next, (t + ahead) % RING_SLOTS):
                cp.start()

        for cp in chunk_copies(c, slot):
            cp.wait()
        g = jnp.dot(f_ref[...], w_buf[slot, 0], preferred_element_type=F32)
        up = jnp.dot(f_ref[0:tm, :], w_buf[slot, 1], preferred_element_type=F32)
        cols = pl.ds(pl.multiple_of(c * tc, tc), tc)
        cw = cw_ref[:, cols]
        g_prev = pltpu.roll(g, 1, axis=0)[0:tm, :]
        g_next = pltpu.roll(g, tm + hal - 1, axis=0)[0:tm, :]
        gate = g_prev * cw[0:1, :] + g[0:tm, :] * cw[1:2, :] + g_next * cw[2:3, :] + cb_ref[:, cols]
        h_ref[:, cols] = (_gelu_tanh(gate) * up).astype(BF16)
        return carry

    lax.fori_loop(0, n_chunks, chunk, 0)


def _gate_up(x1, g_pre, w_gate, w_up, conv_w, conv_b, seq, tm, tc):
    n, d = x1.shape
    dff = w_gate.shape[1]
    hal = CONV_HALO
    nb8 = tm // 8
    tiles_per_seq = seq // tm
    return pl.pallas_call(
        functools.partial(_gate_up_kernel, tm=tm, tc=tc, n_chunks=dff // tc, n_tiles=n // tm,
                          tiles_per_seq=tiles_per_seq),
        grid=(n // tm,),
        in_specs=[
            pl.BlockSpec((tm, d), lambda i: (i, 0)),
            pl.BlockSpec((8, d), lambda i: (jnp.maximum(i * nb8 - 1, 0), 0)),
            pl.BlockSpec((8, d), lambda i: (jnp.minimum((i + 1) * nb8, n // 8 - 1), 0)),
            pl.BlockSpec((1, d), lambda i: (0, 0)),
            pl.BlockSpec((3, dff), lambda i: (0, 0)),
            pl.BlockSpec((1, dff), lambda i: (0, 0)),
            pl.BlockSpec(memory_space=pl.ANY),
            pl.BlockSpec(memory_space=pl.ANY),
        ],
        out_specs=pl.BlockSpec((tm, dff), lambda i: (i, 0)),
        out_shape=jax.ShapeDtypeStruct((n, dff), BF16),
        scratch_shapes=[pltpu.VMEM((tm + hal, d), BF16), pltpu.VMEM((RING_SLOTS, 2, d, tc), BF16),
                        pltpu.SemaphoreType.DMA((RING_SLOTS, 2))],
        compiler_params=pltpu.CompilerParams(
            dimension_semantics=("arbitrary",), vmem_limit_bytes=V7X_VMEM_LIMIT),
        name="ffn_gate_up",
    )(x1, x1, x1, g_pre, conv_w, conv_b, w_gate, w_up)


def _down_kernel(w_ref, x_ref, g_ref, h_hbm, o_ref, h_buf, sem, *, tm, n_tiles):
    i = pl.program_id(0)
    ahead = DOWN_SLOTS - 1

    def tile_copy(t, slot):
        return pltpu.make_async_copy(h_hbm.at[pl.ds(pl.multiple_of(t * tm, tm), tm), :], h_buf.at[slot],
                                     sem.at[slot])

    @pl.when(i == 0)
    def _():
        for t0 in range(ahead):
            tile_copy(t0, t0).start()

    @pl.when(i + ahead < n_tiles)
    def _():
        tile_copy(i + ahead, (i + ahead) % DOWN_SLOTS).start()

    slot = i % DOWN_SLOTS
    tile_copy(i, slot).wait()
    half = tm // 2
    for r0 in (0, half):
        rows = slice(r0, r0 + half)
        y = jnp.dot(h_buf[slot, rows, :], w_ref[...], preferred_element_type=F32)
        o_ref[rows, :] = x_ref[rows, :] + _rms(y, g_ref[...])


def _down(h, w_down, x1, g_post, tm):
    n, d = x1.shape
    dff = h.shape[1]
    return pl.pallas_call(
        functools.partial(_down_kernel, tm=tm, n_tiles=n // tm),
        grid=(n // tm,),
        in_specs=[
            pl.BlockSpec((dff, d), lambda i: (0, 0), pipeline_mode=pl.Buffered(1)),
            pl.BlockSpec((tm, d), lambda i: (i, 0)),
            pl.BlockSpec((1, d), lambda i: (0, 0)),
            pl.BlockSpec(memory_space=pl.ANY),
        ],
        out_specs=pl.BlockSpec((tm, d), lambda i: (i, 0)),
        out_shape=jax.ShapeDtypeStruct((n, d), F32),
        scratch_shapes=[pltpu.VMEM((DOWN_SLOTS, tm, dff), BF16), pltpu.SemaphoreType.DMA((DOWN_SLOTS,))],
        compiler_params=pltpu.CompilerParams(
            dimension_semantics=("arbitrary",), vmem_limit_bytes=V7X_VMEM_LIMIT),
        name="ffn_down",
    )(w_down, x1, g_post, h)


def kernel(x, norm_pre_mix, w_in, w_pool, pool_scale, attn_sink, w_out, norm_post_mix, norm_pre_ffn,
           w_gate, w_up, conv_w, conv_b, w_down, norm_post_ffn):
    b, s, d = x.shape
    depth = w_in.shape[0]
    pool_width = pool_scale.shape[1]
    n_heads = attn_sink.shape[1]
    attn_width = n_heads * HEAD_DIM
    x2 = x.reshape(b * s, d)
    for l in range(depth):
        y_pool, qt, k, vt, w_out_bf, w_gate_bf = _in_proj_pool(
            x2, norm_pre_mix[l][None, :], w_in[l].astype(BF16), w_pool[l].astype(BF16), pool_scale[l][None, :],
            w_out[l], w_gate[l], s, pool_width, attn_width, tm=ROW_TILE)
        y_attn, w_up_bf, w_down_bf = _attention(qt, k.reshape(b, s, -1), vt, attn_sink[l], w_up[l], w_down[l],
                                                tq=ROW_TILE)
        x2 = _out_proj(y_pool, y_attn.reshape(b * s, attn_width), w_out_bf, x2, norm_post_mix[l][None, :], tm=ROW_TILE)
        hmid = _gate_up(x2, norm_pre_ffn[l][None, :], w_gate_bf, w_up_bf, conv_w[l], conv_b[l][None, :], seq=s,
                        tm=ROW_TILE, tc=FFN_CHUNK)
        x2 = _down(hmid, w_down_bf, x2, norm_post_ffn[l][None, :], tm=DOWN_TILE)
    return x2.reshape(b, s, d)
```

```python
import functools

import jax
import jax.numpy as jnp
import numpy as np
from jax import lax
from jax.experimental import pallas as pl
from jax.experimental.pallas import tpu as pltpu

F32 = jnp.float32
BF16 = jnp.bfloat16

EPS = 1e-6
LOG2E = 1.4426950408889634
POOL_WINDOWS = (2, 4, 8, 16)
HEAD_DIM = 64
Q_PER_KV = 4
WINDOW = 128
QBLK = 128
KWIN = 3 * QBLK
SUM_ROWS = 16
IN_HALO = 16
CONV_HALO = 16
FFN_CHUNK = 512
RING_SLOTS = 3
ROW_TILE = 1024
DOWN_TILE = 512
V7X_VMEM_LIMIT = 62 * 1024 * 1024


def _rms(x, g):
    return x * lax.rsqrt(jnp.mean(x * x, axis=-1, keepdims=True) + EPS) * g


def _roll_rows(a, shift):
    return pltpu.roll(a, shift % a.shape[0], axis=0)


def _inproj_pool_kernel(x_ref, xp_ref, xn_ref, g_ref, w_ref, wp_ref, ps_ref, ca_ref, cb_ref,
                        yp_ref, qt_ref, k_ref, vt_ref, ca_out_ref, cb_out_ref, h_ref,
                        *, tm, seq, pool_width, attn_width):
    ca_out_ref[...] = ca_ref[...].astype(BF16)
    cb_out_ref[...] = cb_ref[...].astype(BF16)
    i = pl.program_id(0)
    tiles_per_seq = seq // tm
    it = i % tiles_per_seq
    hal = IN_HALO
    half = tm // 2
    rows = tm + 2 * hal
    gw = pool_width // len(POOL_WINDOWS)
    gp = g_ref[...]

    h_ref[0:hal, :] = jnp.where(it > 0, _rms(xp_ref[...], gp), 0.0).astype(BF16)
    h_ref[hal:hal + half, :] = _rms(x_ref[0:half, :], gp).astype(BF16)
    h_ref[hal + half:hal + tm, :] = _rms(x_ref[half:tm, :], gp).astype(BF16)
    h_ref[hal + tm:, :] = jnp.where(it < tiles_per_seq - 1, _rms(xn_ref[...], gp), 0.0).astype(BF16)

    w_u = w_ref[:, :pool_width]
    u = jnp.concatenate([jnp.dot(h_ref[0:hal + half, :], w_u, preferred_element_type=F32),
                         jnp.dot(h_ref[hal + half:rows, :], w_u, preferred_element_type=F32)], axis=0)
    w_q = w_ref[:, pool_width:pool_width + attn_width]
    w_kv = w_ref[:, pool_width + attn_width:]
    kv_width = k_ref.shape[1]
    grp_width = Q_PER_KV * HEAD_DIM
    for r0 in (0, half):
        hm = h_ref[hal + r0:hal + r0 + half, :]
        q = jnp.dot(hm, w_q, preferred_element_type=F32) * (HEAD_DIM ** -0.5 * LOG2E)
        for jb in range(half // QBLK):
            for hk in range(attn_width // grp_width):
                t = q[jb * QBLK:(jb + 1) * QBLK, hk * grp_width:(hk + 1) * grp_width].T
                qt = jnp.concatenate([t[g * HEAD_DIM:(g + 1) * HEAD_DIM, :] for g in range(Q_PER_KV)], axis=1)
                qt_ref[r0 // QBLK + jb, hk] = qt.astype(BF16)
        kv = jnp.dot(hm, w_kv, preferred_element_type=F32)
        k_ref[r0:r0 + half, :] = kv[:, :kv_width].astype(BF16)
        vt_ref[:, r0:r0 + half] = kv[:, kv_width:].T.astype(BF16)

    a1 = u + _roll_rows(u, 1)
    a1r = a1[:, gw:]
    a2 = _roll_rows(a1r, -1) + _roll_rows(a1r, 1)
    a2r = a2[:, gw:]
    a4 = _roll_rows(a2r, -2) + _roll_rows(a2r, 2)
    a4r = a4[:, gw:]
    a8 = _roll_rows(a4r, -4) + _roll_rows(a4r, 4)
    main = slice(hal, hal + tm)
    wins = (a1[main, :gw], a2[main, :gw], a4[main, :gw], a8[main, :])
    t = it * tm + lax.broadcasted_iota(jnp.int32, (tm, 1), 0)
    for g, w in enumerate(POOL_WINDOWS):
        lo = jnp.maximum(t - w // 2, 0)
        hi = jnp.minimum(t + w // 2, seq)
        inv_cnt = 1.0 / (hi - lo).astype(F32)
        cols = slice(g * gw, (g + 1) * gw)
        d = (wins[g] * inv_cnt - u[main, cols]).astype(BF16)
        y = jnp.dot(d, wp_ref[g], preferred_element_type=F32) * ps_ref[:, cols]
        yp_ref[:, cols] = y.astype(BF16)


def _in_proj_pool(x2, g, w_in, w_pool, pool_scale, cast_a, cast_b, seq, pool_width, attn_width, tm):
    n, d = x2.shape
    in_width = w_in.shape[1]
    ng = len(POOL_WINDOWS)
    gw = pool_width // ng
    hal = IN_HALO
    nh = tm // hal
    steps = n // tm
    kv_width = (in_width - pool_width - attn_width) // 2
    n_kv = attn_width // (Q_PER_KV * HEAD_DIM)
    resident = dict(pipeline_mode=pl.Buffered(1))
    slab_a = pl.BlockSpec((cast_a.shape[0] // steps, cast_a.shape[1]), lambda i: (i, 0))
    slab_b = pl.BlockSpec((cast_b.shape[0] // steps, cast_b.shape[1]), lambda i: (i, 0))
    return pl.pallas_call(
        functools.partial(_inproj_pool_kernel, tm=tm, seq=seq, pool_width=pool_width, attn_width=attn_width),
        grid=(n // tm,),
        in_specs=[
            pl.BlockSpec((tm, d), lambda i: (i, 0)),
            pl.BlockSpec((hal, d), lambda i: (jnp.maximum(i * nh - 1, 0), 0)),
            pl.BlockSpec((hal, d), lambda i: (jnp.minimum((i + 1) * nh, n // hal - 1), 0)),
            pl.BlockSpec((1, d), lambda i: (0, 0)),
            pl.BlockSpec((d, in_width), lambda i: (0, 0), **resident),
            pl.BlockSpec((ng, gw, gw), lambda i: (0, 0, 0), **resident),
            pl.BlockSpec((1, pool_width), lambda i: (0, 0)),
            slab_a,
            slab_b,
        ],
        out_specs=[
            pl.BlockSpec((tm, pool_width), lambda i: (i, 0)),
            pl.BlockSpec((tm // QBLK, n_kv, HEAD_DIM, Q_PER_KV * QBLK), lambda i: (i, 0, 0, 0)),
            pl.BlockSpec((tm, kv_width), lambda i: (i, 0)),
            pl.BlockSpec((kv_width, tm), lambda i: (0, i)),
            slab_a,
            slab_b,
        ],
        out_shape=[
            jax.ShapeDtypeStruct((n, pool_width), BF16),
            jax.ShapeDtypeStruct((n // QBLK, n_kv, HEAD_DIM, Q_PER_KV * QBLK), BF16),
            jax.ShapeDtypeStruct((n, kv_width), BF16),
            jax.ShapeDtypeStruct((kv_width, n), BF16),
            jax.ShapeDtypeStruct(cast_a.shape, BF16),
            jax.ShapeDtypeStruct(cast_b.shape, BF16),
        ],
        scratch_shapes=[pltpu.VMEM((tm + 2 * hal, d), BF16)],
        compiler_params=pltpu.CompilerParams(
            dimension_semantics=("parallel",), vmem_limit_bytes=V7X_VMEM_LIMIT),
        name="in_proj_pool",
    )(x2, x2, x2, g, w_in, w_pool, pool_scale, cast_a, cast_b)


def _attn_kernel(sink_ref, qt_ref, k_ref, vt_ref, ca_ref, cb_ref, o_ref, ca_out_ref, cb_out_ref,
                 bias_ref, sinkrow_ref, st_ref, *, tq, seq, slopes):
    ca_out_ref[...] = ca_ref[...].astype(BF16)
    cb_out_ref[...] = cb_ref[...].astype(BF16)
    bi = pl.program_id(0)
    i = pl.program_id(1)
    grp = Q_PER_KV
    n_heads = len(slopes)
    n_kv = n_heads // grp

    @pl.when((bi == 0) & (i == 0))
    def _():
        kj = lax.broadcasted_iota(jnp.int32, (KWIN, QBLK), 0)
        qi = lax.broadcasted_iota(jnp.int32, (KWIN, QBLK), 1)
        for v in range(3):
            dist_i = jnp.abs(v * QBLK + qi - kj)
            valid = dist_i <= WINDOW
            dist = dist_i.astype(F32)
            for h in range(n_heads):
                cols = slice((h % grp) * QBLK, (h % grp + 1) * QBLK)
                bias_ref[v, h // grp, :, cols] = jnp.where(valid, -(slopes[h] * dist) * LOG2E, -jnp.inf)
        for h in range(n_heads):
            cols = slice((h % grp) * QBLK, (h % grp + 1) * QBLK)
            sinkrow_ref[h // grp, :, cols] = jnp.full((1, QBLK), sink_ref[h] * LOG2E, F32)

    def sub_block(j, carry):
        q0 = i * tq + j * QBLK
        k0 = pl.multiple_of(jnp.clip(q0 - WINDOW, 0, seq - KWIN), QBLK)
        variant = (q0 - k0) // QBLK
        rows = pl.ds(pl.multiple_of(j * QBLK, QBLK), QBLK)
        par = (i + j) % 2
        m_all = []
        for hk in range(n_kv):
            k = k_ref[0, pl.ds(k0, KWIN), hk * HEAD_DIM:(hk + 1) * HEAD_DIM]
            st = jnp.dot(k, qt_ref[j, hk], preferred_element_type=F32)
            st_ref[par, hk] = st + bias_ref[variant, hk]
            m_all.append(jnp.maximum(jnp.max(st_ref[par, hk], axis=0, keepdims=True), sinkrow_ref[hk]))
        ones = jnp.ones((SUM_ROWS, KWIN), BF16)
        for hk in range(n_kv):
            e = jnp.exp2(st_ref[par, hk] - m_all[hk]).astype(BF16)
            vt = jnp.concatenate([vt_ref[hk * HEAD_DIM:(hk + 1) * HEAD_DIM, pl.ds(k0, KWIN)], ones], axis=0)
            ot = jnp.dot(vt, e, preferred_element_type=F32)
            den = ot[HEAD_DIM:HEAD_DIM + 1, :] + jnp.exp2(sinkrow_ref[hk] - m_all[hk])
            ot = ot[:HEAD_DIM, :] * (1.0 / den)
            o4 = jnp.concatenate([ot[:, g * QBLK:(g + 1) * QBLK].T for g in range(grp)], axis=1)
            o_ref[0, rows, hk * grp * HEAD_DIM:(hk + 1) * grp * HEAD_DIM] = o4.astype(BF16)
        return carry

    for j in range(tq // QBLK):
        sub_block(j, 0)


def _attention(qt, k3, vt, sink, cast_a, cast_b, tq):
    b, s, kv_width = k3.shape
    n_kv = qt.shape[1]
    attn_width = n_kv * Q_PER_KV * HEAD_DIM
    steps = b * (s // tq)
    slab_a = pl.BlockSpec((cast_a.shape[0] // steps, cast_a.shape[1]), lambda bi, i: (bi * (s // tq) + i, 0))
    slab_b = pl.BlockSpec((cast_b.shape[0] // steps, cast_b.shape[1]), lambda bi, i: (bi * (s // tq) + i, 0))
    n_heads = attn_width // HEAD_DIM
    idx = np.arange(1, n_heads + 1, dtype=np.float32)
    slopes = tuple(float(v) for v in np.power(2.0, -8.0 * idx / n_heads).astype(np.float32))
    return pl.pallas_call(
        functools.partial(_attn_kernel, tq=tq, seq=s, slopes=slopes),
        grid=(b, s // tq),
        in_specs=[
            pl.BlockSpec(memory_space=pltpu.SMEM),
            pl.BlockSpec((tq // QBLK, n_kv, HEAD_DIM, Q_PER_KV * QBLK), lambda bi, i: (bi * (s // tq) + i, 0, 0, 0)),
            pl.BlockSpec((1, s, kv_width), lambda bi, i: (bi, 0, 0)),
            pl.BlockSpec((kv_width, s), lambda bi, i: (0, bi)),
            slab_a,
            slab_b,
        ],
        out_specs=[pl.BlockSpec((1, tq, attn_width), lambda bi, i: (bi, i, 0)), slab_a, slab_b],
        out_shape=[jax.ShapeDtypeStruct((b, s, attn_width), BF16),
                   jax.ShapeDtypeStruct(cast_a.shape, BF16),
                   jax.ShapeDtypeStruct(cast_b.shape, BF16)],
        scratch_shapes=[
            pltpu.VMEM((3, n_kv, KWIN, Q_PER_KV * QBLK), F32),
            pltpu.VMEM((n_kv, 1, Q_PER_KV * QBLK), F32),
            pltpu.VMEM((2, n_kv, KWIN, Q_PER_KV * QBLK), F32),
        ],
        compiler_params=pltpu.CompilerParams(
            dimension_semantics=("arbitrary", "arbitrary"), vmem_limit_bytes=V7X_VMEM_LIMIT),
        name="window_attn",
    )(sink, qt, k3, vt, cast_a, cast_b)


def _outproj_kernel(yp_ref, ya_ref, w_ref, x_ref, g_ref, o_ref, *, pool_width, tm):
    half = tm // 2
    for r0 in (0, half):
        rows = slice(r0, r0 + half)
        mix = jnp.dot(yp_ref[rows, :], w_ref[:pool_width, :], preferred_element_type=F32)
        mix = mix + jnp.dot(ya_ref[rows, :], w_ref[pool_width:, :], preferred_element_type=F32)
        o_ref[rows, :] = x_ref[rows, :] + _rms(mix, g_ref[...])


def _out_proj(yp, ya, w_out, x2, g, tm):
    n, d = x2.shape
    pw = yp.shape[1]
    aw = ya.shape[1]
    return pl.pallas_call(
        functools.partial(_outproj_kernel, pool_width=pw, tm=tm),
        grid=(n // tm,),
        in_specs=[
            pl.BlockSpec((tm, pw), lambda i: (i, 0)),
            pl.BlockSpec((tm, aw), lambda i: (i, 0)),
            pl.BlockSpec((pw + aw, d), lambda i: (0, 0), pipeline_mode=pl.Buffered(1)),
            pl.BlockSpec((tm, d), lambda i: (i, 0)),
            pl.BlockSpec((1, d), lambda i: (0, 0)),
        ],
        out_specs=pl.BlockSpec((tm, d), lambda i: (i, 0)),
        out_shape=jax.ShapeDtypeStruct((n, d), F32),
        compiler_params=pltpu.CompilerParams(
            dimension_semantics=("parallel",), vmem_limit_bytes=V7X_VMEM_LIMIT),
        name="out_proj",
    )(yp, ya, w_out, x2, g)


def _gelu_tanh(x):
    return 0.5 * x * (1.0 + jnp.tanh(np.sqrt(2.0 / np.pi).astype(np.float32) * (x + 0.044715 * (x * x * x))))


def _gate_up_kernel(x_ref, xp_ref, xn_ref, gpre_ref, cw_ref, cb_ref, wg_hbm, wu_hbm, h_ref, f_ref, w_buf, sem,
                    *, tm, tc, n_chunks, n_tiles, tiles_per_seq):
    i = pl.program_id(0)
    hal = CONV_HALO

    def chunk_copies(c, slot):
        cols = pl.ds(pl.multiple_of(c * tc, tc), tc)
        return (pltpu.make_async_copy(wg_hbm.at[:, cols], w_buf.at[slot, 0], sem.at[slot, 0]),
                pltpu.make_async_copy(wu_hbm.at[:, cols], w_buf.at[slot, 1], sem.at[slot, 1]))

    @pl.when(i == 0)
    def _():
        for c0 in range(RING_SLOTS - 1):
            for cp in chunk_copies(c0, c0):
                cp.start()

    it = i % tiles_per_seq
    gp = gpre_ref[...]
    f_ref[0:tm, :] = _rms(x_ref[...], gp).astype(BF16)
    edge = jnp.concatenate([_rms(xn_ref[...], gp), _rms(xp_ref[...], gp)], axis=0)
    row = lax.broadcasted_iota(jnp.int32, (hal, 1), 0)
    next_row = jnp.where(it < tiles_per_seq - 1, 0, -1)
    prev_row = jnp.where(it > 0, hal - 1, -1)
    f_ref[tm:, :] = jnp.where((row == next_row) | (row == prev_row), edge, 0.0).astype(BF16)

    def chunk(c, carry):
        t = i * n_chunks + c
        slot = t % RING_SLOTS
        ahead = RING_SLOTS - 1

        @pl.when(t + ahead < n_tiles * n_chunks)
        def _():
            c_next = jnp.where(c + ahead >= n_chunks, c + ahead - n_chunks, c + ahead)
            for prio, cp in enumerate(chunk_copies(c_next, (t + ahead) % RING_SLOTS)):
                cp.start(priority=prio)

        for cp in chunk_copies(c, slot):
            cp.wait()
        g = jnp.dot(f_ref[...], w_buf[slot, 0], preferred_element_type=F32)
        up = jnp.dot(f_ref[0:tm, :], w_buf[slot, 1], preferred_element_type=F32)
        cols = pl.ds(pl.multiple_of(c * tc, tc), tc)
        cw = cw_ref[:, cols]
        g_prev = pltpu.roll(g, 1, axis=0)[0:tm, :]
        g_next = pltpu.roll(g, tm + hal - 1, axis=0)[0:tm, :]
        gate = g_prev * cw[0:1, :] + g[0:tm, :] * cw[1:2, :] + g_next * cw[2:3, :] + cb_ref[:, cols]
        h_ref[:, cols] = (_gelu_tanh(gate) * up).astype(BF16)
        return carry

    lax.fori_loop(0, n_chunks, chunk, 0)


def _gate_up(x1, g_pre, w_gate, w_up, conv_w, conv_b, seq, tm, tc):
    n, d = x1.shape
    dff = w_gate.shape[1]
    hal = CONV_HALO
    nb8 = tm // 8
    tiles_per_seq = seq // tm
    return pl.pallas_call(
        functools.partial(_gate_up_kernel, tm=tm, tc=tc, n_chunks=dff // tc, n_tiles=n // tm,
                          tiles_per_seq=tiles_per_seq),
        grid=(n // tm,),
        in_specs=[
            pl.BlockSpec((tm, d), lambda i: (i, 0)),
            pl.BlockSpec((8, d), lambda i: (jnp.maximum(i * nb8 - 1, 0), 0)),
            pl.BlockSpec((8, d), lambda i: (jnp.minimum((i + 1) * nb8, n // 8 - 1), 0)),
            pl.BlockSpec((1, d), lambda i: (0, 0)),
            pl.BlockSpec((3, dff), lambda i: (0, 0)),
            pl.BlockSpec((1, dff), lambda i: (0, 0)),
            pl.BlockSpec(memory_space=pl.ANY),
            pl.BlockSpec(memory_space=pl.ANY),
        ],
        out_specs=pl.BlockSpec((tm, dff), lambda i: (i, 0)),
        out_shape=jax.ShapeDtypeStruct((n, dff), BF16),
        scratch_shapes=[pltpu.VMEM((tm + hal, d), BF16), pltpu.VMEM((RING_SLOTS, 2, d, tc), BF16),
                        pltpu.SemaphoreType.DMA((RING_SLOTS, 2))],
        compiler_params=pltpu.CompilerParams(
            dimension_semantics=("arbitrary",), vmem_limit_bytes=V7X_VMEM_LIMIT),
        name="ffn_gate_up",
    )(x1, x1, x1, g_pre, conv_w, conv_b, w_gate, w_up)


def _down_kernel(h_ref, w_ref, x_ref, g_ref, o_ref, *, tm):
    half = tm // 2
    for r0 in (0, half):
        rows = slice(r0, r0 + half)
        y = jnp.dot(h_ref[rows, :], w_ref[...], preferred_element_type=F32)
        o_ref[rows, :] = x_ref[rows, :] + _rms(y, g_ref[...])


def _down(h, w_down, x1, g_post, tm):
    n, d = x1.shape
    dff = h.shape[1]
    return pl.pallas_call(
        functools.partial(_down_kernel, tm=tm),
        grid=(n // tm,),
        in_specs=[
            pl.BlockSpec((tm, dff), lambda i: (i, 0)),
            pl.BlockSpec((dff, d), lambda i: (0, 0), pipeline_mode=pl.Buffered(1)),
            pl.BlockSpec((tm, d), lambda i: (i, 0)),
            pl.BlockSpec((1, d), lambda i: (0, 0)),
        ],
        out_specs=pl.BlockSpec((tm, d), lambda i: (i, 0)),
        out_shape=jax.ShapeDtypeStruct((n, d), F32),
        compiler_params=pltpu.CompilerParams(
            dimension_semantics=("parallel",), vmem_limit_bytes=V7X_VMEM_LIMIT),
        name="ffn_down",
    )(h, w_down, x1, g_post)


def kernel(x, norm_pre_mix, w_in, w_pool, pool_scale, attn_sink, w_out, norm_post_mix, norm_pre_ffn,
           w_gate, w_up, conv_w, conv_b, w_down, norm_post_ffn):
    b, s, d = x.shape
    depth = w_in.shape[0]
    pool_width = pool_scale.shape[1]
    n_heads = attn_sink.shape[1]
    attn_width = n_heads * HEAD_DIM
    x2 = x.reshape(b * s, d)
    for l in range(depth):
        y_pool, qt, k, vt, w_out_bf, w_gate_bf = _in_proj_pool(
            x2, norm_pre_mix[l][None, :], w_in[l].astype(BF16), w_pool[l].astype(BF16), pool_scale[l][None, :],
            w_out[l], w_gate[l], s, pool_width, attn_width, tm=ROW_TILE)
        y_attn, w_up_bf, w_down_bf = _attention(qt, k.reshape(b, s, -1), vt, attn_sink[l], w_up[l], w_down[l],
                                                tq=ROW_TILE)
        x2 = _out_proj(y_pool, y_attn.reshape(b * s, attn_width), w_out_bf, x2, norm_post_mix[l][None, :], tm=ROW_TILE)
        hmid = _gate_up(x2, norm_pre_ffn[l][None, :], w_gate_bf, w_up_bf, conv_w[l], conv_b[l][None, :], seq=s,
                        tm=ROW_TILE, tc=FFN_CHUNK)
        x2 = _down(hmid, w_down_bf, x2, norm_post_ffn[l][None, :], tm=DOWN_TILE)
    return x2.reshape(b, s, d)
```
